```python
import math
import jax, jax.numpy as jnp
from jax import lax
import numpy as np

D_MODEL = 1024
BATCH = 8
SEQ = 4096
DEPTH = 1

CHUNK = 64
RMS_EPS = 1e-6
GN_EPS = 1e-5
RET_HEAD_DIM = 256
RET_HEADS = D_MODEL // RET_HEAD_DIM
RET_WIDTH = RET_HEADS * RET_HEAD_DIM
RET_THETA = 10000.0
DIFF_HEAD_DIM = 64
DIFF_HEADS = D_MODEL // (2 * DIFF_HEAD_DIM)
DIFF_QK_WIDTH = 2 * DIFF_HEADS * DIFF_HEAD_DIM
DIFF_V_WIDTH = DIFF_HEADS * 2 * DIFF_HEAD_DIM
DIFF_ROT_DIM = DIFF_HEAD_DIM // 4
ROPE_THETA = 500000.0
Q_BLOCK = 128
SPLIT_SIZES = (RET_WIDTH, RET_WIDTH, RET_WIDTH, RET_WIDTH,
               DIFF_QK_WIDTH, DIFF_QK_WIDTH, DIFF_V_WIDTH, DIFF_V_WIDTH,
               D_MODEL, D_MODEL)
IN_WIDTH = sum(SPLIT_SIZES)
SPLIT_POINTS = tuple(int(s) for s in np.cumsum(SPLIT_SIZES)[:-1])

kernel_name = 'hybrid_retention_diffattn_gated_block'


def rms_norm(x, g):
    xf = x.astype(jnp.float32)
    y = xf * lax.rsqrt(jnp.mean(xf * xf, axis=-1, keepdims=True) + RMS_EPS)
    return (y * g.astype(jnp.float32)).astype(x.dtype)


def rope(x, pos, rot_dim, theta):
    half = rot_dim // 2
    inv_freq = theta ** (-jnp.arange(half, dtype=jnp.float32) * 2.0 / rot_dim)
    ang = pos.astype(jnp.float32)[:, :, None] * inv_freq
    cos = jnp.cos(ang)[:, :, None, :].astype(x.dtype)
    sin = jnp.sin(ang)[:, :, None, :].astype(x.dtype)
    x1 = x[..., :half]
    x2 = x[..., half:rot_dim]
    return jnp.concatenate([x1 * cos - x2 * sin, x2 * cos + x1 * sin, x[..., rot_dim:]], axis=-1)


def retention(q, k, v, pos, norm_g):
    B, S, H, d = q.shape
    N = S // CHUNK
    f32 = jnp.float32
    q = rope(q.astype(f32), pos, d, RET_THETA)
    k = rope(k.astype(f32), pos, d, RET_THETA) * (d ** -0.5)
    v = v.astype(f32)
    log_gamma = jnp.log1p(-jnp.exp(jnp.linspace(math.log(1.0 / 32), math.log(1.0 / 512), H, dtype=f32)))
    idx = jnp.arange(CHUNK, dtype=f32)
    decay_intra = jnp.exp(jnp.abs(idx[:, None] - idx[None, :])[None] * log_gamma[:, None, None])
    xi = jnp.exp((idx + 1.0)[None] * log_gamma[:, None])
    zeta = jnp.exp((CHUNK - 1.0 - idx)[None] * log_gamma[:, None])
    g_chunk = jnp.exp(CHUNK * log_gamma)

    def to_chunks(t):
        return t.reshape(B, N, CHUNK, H, d).transpose(1, 0, 3, 2, 4)

    qc, kc, vc = to_chunks(q), to_chunks(k), to_chunks(v)
    scores = jnp.einsum('nbhcd,nbhmd->nbhcm', qc, kc) * decay_intra[None, None]
    intra = jnp.einsum('nbhcm,nbhmd->nbhcd', scores, vc)

    def step(state, inp):
        q_n, k_n, v_n = inp
        inter = jnp.einsum('bhcd,bhde->bhce', q_n, state) * xi[None, :, :, None]
        state = state * g_chunk[None, :, None, None] + jnp.einsum(
            'bhcd,bhce->bhde', k_n * zeta[None, :, :, None], v_n)
        return state, inter

    state0 = jnp.zeros((B, H, d, d), f32)
    _, inter = lax.scan(step, state0, (qc, kc, vc))
    o = intra + inter
    mu = jnp.mean(o, axis=-1, keepdims=True)
    var = jnp.mean(jnp.square(o - mu), axis=-1, keepdims=True)
    o = (o - mu) * lax.rsqrt(var + GN_EPS)
    o = o.transpose(1, 0, 3, 2, 4).reshape(B, S, H * d) * norm_g.astype(f32)
    return o


def diff_attention(q, k, v, pos, lq1, lk1, lq2, lk2, subln_g, lambda_init):
    B, S, _ = q.shape
    H, hd = DIFF_HEADS, DIFF_HEAD_DIM
    f32 = jnp.float32
    q = rope(q.reshape(B, S, 2 * H, hd), pos, DIFF_ROT_DIM, ROPE_THETA) * (hd ** -0.5)
    k = rope(k.reshape(B, S, 2 * H, hd), pos, DIFF_ROT_DIM, ROPE_THETA)
    q = q.transpose(0, 2, 1, 3)
    k = k.transpose(0, 2, 1, 3)
    v = v.reshape(B, S, H, 2 * hd).transpose(0, 2, 1, 3)
    lam = (jnp.exp(jnp.sum(lq1.astype(f32) * lk1.astype(f32)))
           - jnp.exp(jnp.sum(lq2.astype(f32) * lk2.astype(f32))) + lambda_init)
    key_chunk = jnp.arange(S) // CHUNK

    def attend_block(blk):
        start = blk * Q_BLOCK
        qb = lax.dynamic_slice_in_dim(q, start, Q_BLOCK, axis=2)
        s = jnp.einsum('bhqd,bhkd->bhqk', qb, k).astype(f32)
        q_chunk = (start + jnp.arange(Q_BLOCK)) // CHUNK
        s = jnp.where(key_chunk[None, :] <= q_chunk[:, None], s, -jnp.inf)
        p = jax.nn.softmax(s, axis=-1).reshape(B, H, 2, Q_BLOCK, S)
        a = p[:, :, 0] - lam * p[:, :, 1]
        return jnp.einsum('bhqk,bhkd->bhqd', a.astype(v.dtype), v)

    o = lax.map(attend_block, jnp.arange(S // Q_BLOCK))
    o = o.transpose(1, 0, 3, 2, 4).reshape(B, S, H, 2 * hd)
    o = rms_norm(o, subln_g) * (1.0 - lambda_init)
    return o.reshape(B, S, H * 2 * hd)


def setup_inputs(seed: int = 0) -> dict:
    key = jax.random.key(seed)
    ks = jax.random.split(key, 20)
    f32 = jnp.float32
    D = D_MODEL

    def nrm(k, shape, scale):
        return jax.random.normal(k, shape, f32) * scale

    x = jax.random.normal(ks[0], (BATCH, SEQ, D), f32)
    c = jax.random.normal(ks[1], (BATCH, D), f32)
    offsets = jax.random.randint(ks[2], (BATCH, 1), 0, 64, dtype=jnp.int32) * CHUNK
    positions = (offsets + jnp.arange(SEQ, dtype=jnp.int32)[None, :]).astype(jnp.int32)
    return {
        'x': x,
        'c': c,
        'positions': positions,
        'ada_w': nrm(ks[3], (DEPTH, D, 3 * D), D ** -0.5),
        'ada_b': nrm(ks[4], (DEPTH, 3 * D), 0.01),
        'pre_norm_g': 1.0 + nrm(ks[5], (DEPTH, D), 0.02),
        'w_in': nrm(ks[6], (DEPTH, D, IN_WIDTH), D ** -0.5),
        'ret_norm_g': 1.0 + nrm(ks[7], (DEPTH, RET_WIDTH), 0.02),
        'diff_lambda_q1': nrm(ks[8], (DEPTH, DIFF_HEAD_DIM), 0.1),
        'diff_lambda_k1': nrm(ks[9], (DEPTH, DIFF_HEAD_DIM), 0.1),
        'diff_lambda_q2': nrm(ks[10], (DEPTH, DIFF_HEAD_DIM), 0.1),
        'diff_lambda_k2': nrm(ks[11], (DEPTH, DIFF_HEAD_DIM), 0.1),
        'diff_subln_g': 1.0 + nrm(ks[12], (DEPTH, 2 * DIFF_HEAD_DIM), 0.02),
        'w_branch_ret': nrm(ks[13], (DEPTH, RET_WIDTH, D), RET_WIDTH ** -0.5),
        'w_branch_diff': nrm(ks[14], (DEPTH, DIFF_V_WIDTH, D), DIFF_V_WIDTH ** -0.5),
        'w_out': nrm(ks[15], (DEPTH, D, D), D ** -0.5),
        'post_norm_g': 1.0 + nrm(ks[16], (DEPTH, D), 0.02),
    }


def reference(x, c, positions, ada_w, ada_b, pre_norm_g, w_in, ret_norm_g,
              diff_lambda_q1, diff_lambda_k1, diff_lambda_q2, diff_lambda_k2,
              diff_subln_g, w_branch_ret, w_branch_diff, w_out, post_norm_g):
    B, S, D = x.shape
    for l in range(DEPTH):
        lambda_init = 0.8 - 0.6 * math.exp(-0.3 * l)
        mod = jax.nn.silu(c) @ ada_w[l] + ada_b[l]
        shift, scale, gate = jnp.split(mod, 3, axis=-1)
        h = rms_norm(x, pre_norm_g[l]) * (1.0 + scale[:, None, :]) + shift[:, None, :]
        proj = h @ w_in[l]
        (rq, rk, rv, rg, dq, dk, dv, dg, m_ret, m_diff) = jnp.split(proj, SPLIT_POINTS, axis=-1)

        y_ret = retention(rq.reshape(B, S, RET_HEADS, RET_HEAD_DIM),
                          rk.reshape(B, S, RET_HEADS, RET_HEAD_DIM),
                          rv.reshape(B, S, RET_HEADS, RET_HEAD_DIM),
                          positions, ret_norm_g[l]).astype(x.dtype)
        y_ret = y_ret * jax.nn.silu(rg)

        y_diff = diff_attention(dq, dk, dv, positions,
                                diff_lambda_q1[l], diff_lambda_k1[l],
                                diff_lambda_q2[l], diff_lambda_k2[l],
                                diff_subln_g[l], lambda_init).astype(x.dtype)
        y_diff = y_diff * jax.nn.silu(dg)

        merged = (jax.nn.sigmoid(m_ret) * (y_ret @ w_branch_ret[l])
                  + jax.nn.sigmoid(m_diff) * (y_diff @ w_branch_diff[l]))
        out = merged @ w_out[l]
        x = x + gate[:, None, :] * rms_norm(out, post_norm_g[l])
    return x
```

```python
import functools
import math

import jax
import jax.numpy as jnp
import numpy as np
from jax import lax
from jax.experimental import pallas as pl
from jax.experimental.pallas import tpu as pltpu

D_MODEL = 1024
CHUNK = 64
RMS_EPS = 1e-6
GN_EPS = 1e-5
RET_HEAD_DIM = 256
RET_HEADS = D_MODEL // RET_HEAD_DIM
RET_THETA = 10000.0
DIFF_HEAD_DIM = 64
DIFF_HEADS = D_MODEL // (2 * DIFF_HEAD_DIM)
DIFF_ROT_DIM = DIFF_HEAD_DIM // 4
ROPE_THETA = 500000.0
N_SECTIONS = 10
LANES = 128
LOG2E = math.log2(math.e)

VMEM_LIMIT_BYTES = 56 * 1024 * 1024

BF16 = jnp.bfloat16
F32 = jnp.float32


def _silu(v):
    return v * jax.nn.sigmoid(v)


def _ada_kernel(c_ref, w_ref, b_ref, o_ref):
    sc = _silu(c_ref[...]).astype(BF16)
    o_ref[...] = jnp.dot(sc, w_ref[...], preferred_element_type=F32) + b_ref[...]


def _ada_mod(c, w_bf16, b):
    bsz, d = c.shape
    n = w_bf16.shape[1]
    tn = 1024
    return pl.pallas_call(
        _ada_kernel,
        out_shape=jax.ShapeDtypeStruct((bsz, n), F32),
        grid=(n // tn,),
        in_specs=[
            pl.BlockSpec((bsz, d), lambda j: (0, 0)),
            pl.BlockSpec((d, tn), lambda j: (0, j)),
            pl.BlockSpec((1, tn), lambda j: (0, j)),
        ],
        out_specs=pl.BlockSpec((bsz, tn), lambda j: (0, j)),
        name="ada_mod",
    )(c, w_bf16, b)


def _in_proj_kernel(x_ref, pos_ref, shift_ref, scale_ref, g_ref, fr_ref, fd_ref, w_ref,
                    o_ref, h_ref, rc_ref, rs_ref, dc_ref, ds1_ref, ds2_ref):
    j = pl.program_id(1)

    @pl.when(j == 0)
    def _prepare():
        x = x_ref[...]
        inv = lax.rsqrt(jnp.mean(x * x, axis=-1, keepdims=True) + RMS_EPS)
        h = (x * inv) * g_ref[...]
        h = h * (1.0 + scale_ref[0]) + shift_ref[0]
        h_ref[...] = h.astype(BF16)
        pos = pos_ref[...].astype(F32)
        ang_r = pos * fr_ref[...]
        rc_ref[...] = jnp.cos(ang_r)
        rs_ref[...] = jnp.sin(ang_r)
        ang_d = pos * fd_ref[...]
        ch = lax.broadcasted_iota(jnp.int32, ang_d.shape, 1) % DIFF_HEAD_DIM
        half = DIFF_ROT_DIM // 2
        cd = jnp.cos(ang_d)
        sd = jnp.sin(ang_d)
        dc_ref[...] = jnp.where(ch < DIFF_ROT_DIM, cd, 1.0)
        ds1_ref[...] = jnp.where(ch < half, -sd, 0.0)
        ds2_ref[...] = jnp.where((ch >= half) & (ch < DIFF_ROT_DIM), sd, 0.0)

    acc = jnp.dot(h_ref[...], w_ref[...], preferred_element_type=F32)

    def _ret_rope(mult):
        c = rc_ref[...]
        s = rs_ref[...]
        hw = RET_HEAD_DIM // 2
        for hd in range(RET_HEADS):
            lo = hd * RET_HEAD_DIM
            x1 = acc[:, lo:lo + hw]
            x2 = acc[:, lo + hw:lo + 2 * hw]
            o_ref[:, lo:lo + hw] = ((x1 * c - x2 * s) * mult).astype(BF16)
            o_ref[:, lo + hw:lo + 2 * hw] = ((x2 * c + x1 * s) * mult).astype(BF16)

    def _diff_rope(mult):
        c = dc_ref[...]
        s1 = ds1_ref[...]
        s2 = ds2_ref[...]
        half = DIFF_ROT_DIM // 2
        for sl in range(D_MODEL // LANES):
            xs = acc[:, sl * LANES:(sl + 1) * LANES]
            up = pltpu.roll(xs, LANES - half, axis=1)
            dn = pltpu.roll(xs, half, axis=1)
            o_ref[:, sl * LANES:(sl + 1) * LANES] = ((xs * c + up * s1 + dn * s2) * mult).astype(BF16)

    @pl.when(j == 0)
    def _rq():
        _ret_rope(1.0)

    @pl.when(j == 1)
    def _rk():
        _ret_rope(RET_HEAD_DIM ** -0.5)

    @pl.when(j == 4)
    def _dq():
        _diff_rope(DIFF_HEAD_DIM ** -0.5 * LOG2E)

    @pl.when(j == 5)
    def _dk():
        _diff_rope(1.0)

    @pl.when((j != 0) & (j != 1) & (j != 4) & (j != 5))
    def _plain():
        o_ref[...] = acc.astype(BF16)


def _in_proj(x2, pos2, shift3, scale3, g2, freq_r, freq_d, w_bf16, *, seq, tm):
    m, d = x2.shape
    n = w_bf16.shape[1]
    tn = D_MODEL
    per_b = seq // tm
    tab = pltpu.VMEM((tm, LANES), F32)
    return pl.pallas_call(
        _in_proj_kernel,
        out_shape=jax.ShapeDtypeStruct((m, n), BF16),
        grid=(m // tm, n // tn),
        in_specs=[
            pl.BlockSpec((tm, d), lambda i, j: (i, 0)),
            pl.BlockSpec((tm, 1), lambda i, j: (i, 0)),
            pl.BlockSpec((1, 1, d), lambda i, j: (i // per_b, 0, 0)),
            pl.BlockSpec((1, 1, d), lambda i, j: (i // per_b, 0, 0)),
            pl.BlockSpec((1, d), lambda i, j: (0, 0)),
            pl.BlockSpec((1, LANES), lambda i, j: (0, 0)),
            pl.BlockSpec((1, LANES), lambda i, j: (0, 0)),
            pl.BlockSpec((d, tn), lambda i, j: (0, j)),
        ],
        out_specs=pl.BlockSpec((tm, tn), lambda i, j: (i, j)),
        scratch_shapes=[pltpu.VMEM((tm, d), BF16), tab, tab, tab, tab, tab],
        compiler_params=pltpu.CompilerParams(
            dimension_semantics=("arbitrary", "arbitrary"),
            vmem_limit_bytes=VMEM_LIMIT_BYTES),
        name="in_proj",
    )(x2, pos2, shift3, scale3, g2, freq_r, freq_d, w_bf16)


def _log_gammas():
    lo, hi = math.log(1.0 / 32), math.log(1.0 / 512)
    return [math.log1p(-math.exp(lo + (hi - lo) * i / (RET_HEADS - 1))) for i in range(RET_HEADS)]


def _retention_kernel(q_ref, k_ref, v_ref, g_ref, ng_ref, o_ref, state_ref, *, tb):
    n = pl.program_id(1)

    @pl.when(n == 0)
    def _reset():
        state_ref[...] = jnp.zeros_like(state_ref)

    row = lax.broadcasted_iota(jnp.int32, (tb, tb), 0)
    col = lax.broadcasted_iota(jnp.int32, (tb, tb), 1)
    dist = jnp.abs(row - col).astype(F32)
    visible = (col // CHUNK) <= (row // CHUNK)
    t_col = lax.broadcasted_iota(jnp.int32, (tb, 1), 0).astype(F32)

    for hd, lg in enumerate(_log_gammas()):
        lo = hd * RET_HEAD_DIM
        q = q_ref[:, lo:lo + RET_HEAD_DIM]
        k = k_ref[:, lo:lo + RET_HEAD_DIM]
        v = v_ref[:, lo:lo + RET_HEAD_DIM]
        decay = jnp.where(visible, jnp.exp(dist * lg), 0.0)
        scores = lax.dot_general(q, k, (((1,), (1,)), ((), ())), preferred_element_type=F32)
        scores = (scores * decay).astype(BF16)
        intra = jnp.dot(scores, v, preferred_element_type=F32)
        state = state_ref[hd]
        xi = jnp.exp((t_col + 1.0) * lg)
        inter = jnp.dot(q, state.astype(BF16), preferred_element_type=F32) * xi
        zeta = jnp.exp((tb - 1.0 - t_col) * lg)
        kz = (k.astype(F32) * zeta).astype(BF16)
        upd = lax.dot_general(kz, v, (((0,), (0,)), ((), ())), preferred_element_type=F32)
        state_ref[hd] = state * math.exp(tb * lg) + upd

        o = intra + inter
        mu = jnp.mean(o, axis=-1, keepdims=True)
        oc = o - mu
        var = jnp.mean(oc * oc, axis=-1, keepdims=True)
        o = oc * lax.rsqrt(var + GN_EPS) * ng_ref[:, lo:lo + RET_HEAD_DIM]
        gate = g_ref[:, lo:lo + RET_HEAD_DIM].astype(F32)
        o_ref[:, lo:lo + RET_HEAD_DIM] = (o * _silu(gate)).astype(BF16)


def _retention(proj, ret_norm_g2, *, bsz, seq, tb):
    m = proj.shape[0]
    nb = seq // tb
    d = D_MODEL

    def sec(k):
        return pl.BlockSpec((tb, d), lambda b, n: (b * nb + n, k))

    return pl.pallas_call(
        functools.partial(_retention_kernel, tb=tb),
        out_shape=jax.ShapeDtypeStruct((m, d), BF16),
        grid=(bsz, nb),
        in_specs=[sec(0), sec(1), sec(2), sec(3), pl.BlockSpec((1, d), lambda b, n: (0, 0))],
        out_specs=pl.BlockSpec((tb, d), lambda b, n: (b * nb + n, 0)),
        scratch_shapes=[pltpu.VMEM((RET_HEADS, RET_HEAD_DIM, RET_HEAD_DIM), F32)],
        compiler_params=pltpu.CompilerParams(
            dimension_semantics=("arbitrary", "arbitrary"),
            vmem_limit_bytes=VMEM_LIMIT_BYTES),
        name="retention",
    )(proj, proj, proj, proj, ret_norm_g2)


NEG_BIG = -1e30


def _diff_attn_kernel(q_ref, k_ref, v_ref, g_ref, lq1_ref, lk1_ref, lq2_ref, lk2_ref, sg_ref,
                      o_ref, acc_ref, m_ref, l_ref, *, tq, tk, lambda_init):
    i = pl.program_id(2)
    hd = DIFF_HEAD_DIM

    q = q_ref[...]
    lane = lax.broadcasted_iota(jnp.int32, q.shape, 1)
    zero = jnp.zeros_like(q)
    qm = (jnp.where(lane < hd, q, zero), jnp.where(lane >= hd, q, zero))

    m_ref[...] = jnp.full(m_ref.shape, NEG_BIG, F32)
    l_ref[...] = jnp.zeros_like(l_ref)
    acc_ref[...] = jnp.zeros_like(acc_ref)

    def tile(j, masked):
        start = pl.multiple_of(j * tk, tk)
        k = k_ref[pl.ds(start, tk), :]
        v = v_ref[pl.ds(start, tk), :]
        if masked:
            kc = (start + lax.broadcasted_iota(jnp.int32, (tk, tq), 0)) // CHUNK
            qc = (i * tq + lax.broadcasted_iota(jnp.int32, (tk, tq), 1)) // CHUNK
            vis = kc <= qc
        for mp in range(2):
            s = lax.dot_general(k, qm[mp], (((1,), (1,)), ((), ())), preferred_element_type=F32)
            if masked:
                s = jnp.where(vis, s, -jnp.inf)
            m_old = m_ref[mp]
            m_new = jnp.maximum(m_old, jnp.max(s, axis=0, keepdims=True))
            alpha = jnp.exp2(m_old - m_new)
            p = jnp.exp2(s - m_new)
            l_ref[mp] = alpha * l_ref[mp] + jnp.sum(p, axis=0, keepdims=True)
            m_ref[mp] = m_new
            pv = lax.dot_general(v, p.astype(BF16), (((0,), (0,)), ((), ())),
                                 preferred_element_type=F32)
            acc_ref[mp] = alpha * acc_ref[mp] + pv

    n_full = (i * tq) // tk

    def body(j, carry):
        tile(j, False)
        return carry

    lax.fori_loop(0, n_full, body, 0)
    for dj in range(tq // tk):
        tile(n_full + dj, True)

    lam = (jnp.exp(jnp.sum(lq1_ref[...] * lk1_ref[...], axis=-1, keepdims=True))
           - jnp.exp(jnp.sum(lq2_ref[...] * lk2_ref[...], axis=-1, keepdims=True))
           + lambda_init)
    o_t = acc_ref[0] / l_ref[0] - lam * (acc_ref[1] / l_ref[1])
    o = o_t.T
    o = o * lax.rsqrt(jnp.mean(o * o, axis=-1, keepdims=True) + RMS_EPS)
    o = o * sg_ref[...] * (1.0 - lambda_init)
    o_ref[...] = (o * _silu(g_ref[...].astype(F32))).astype(BF16)


def _diff_attn(proj, lq1, lk1, lq2, lk2, subln_g2, *, bsz, seq, tq, tk, lambda_init):
    m = proj.shape[0]
    nq = seq // tq
    w = 2 * DIFF_HEAD_DIM
    per_sec = D_MODEL // w

    def q_like(sec_idx):
        return pl.BlockSpec((tq, w), lambda b, h, i: (b * nq + i, sec_idx * per_sec + h))

    def kv_like(sec_idx):
        return pl.BlockSpec((seq, w), lambda b, h, i: (b, sec_idx * per_sec + h))

    small = pl.BlockSpec((1, DIFF_HEAD_DIM), lambda b, h, i: (0, 0))
    return pl.pallas_call(
        functools.partial(_diff_attn_kernel, tq=tq, tk=tk, lambda_init=lambda_init),
        out_shape=jax.ShapeDtypeStruct((m, D_MODEL), BF16),
        grid=(bsz, DIFF_HEADS, nq),
        in_specs=[q_like(4), kv_like(5), kv_like(6), q_like(7), small, small, small, small,
                  pl.BlockSpec((1, w), lambda b, h, i: (0, 0))],
        out_specs=pl.BlockSpec((tq, w), lambda b, h, i: (b * nq + i, h)),
        scratch_shapes=[pltpu.VMEM((2, w, tq), F32), pltpu.VMEM((2, 1, tq), F32),
                        pltpu.VMEM((2, 1, tq), F32)],
        compiler_params=pltpu.CompilerParams(
            dimension_semantics=("arbitrary", "arbitrary", "arbitrary"),
            vmem_limit_bytes=VMEM_LIMIT_BYTES),
        name="diff_attn",
    )(proj, proj, proj, proj, lq1, lk1, lq2, lk2, subln_g2)


def _out_kernel(x_ref, yr_ref, yd_ref, mr_ref, md_ref, gate_ref, pg_ref, wr_ref, wd_ref, wo_ref, o_ref):
    br = jnp.dot(yr_ref[...], wr_ref[...], preferred_element_type=F32)
    bd = jnp.dot(yd_ref[...], wd_ref[...], preferred_element_type=F32)
    merged = (jax.nn.sigmoid(mr_ref[...].astype(F32)) * br
              + jax.nn.sigmoid(md_ref[...].astype(F32)) * bd)
    out = jnp.dot(merged.astype(BF16), wo_ref[...], preferred_element_type=F32)
    normed = out * lax.rsqrt(jnp.mean(out * out, axis=-1, keepdims=True) + RMS_EPS) * pg_ref[...]
    o_ref[...] = x_ref[...] + gate_ref[0] * normed


def _out_merge(x2, y_ret, y_diff, proj, gate3, post_g2, wr, wd, wo, *, seq, tm):
    m, d = x2.shape
    per_b = seq // tm
    row = lambda k: pl.BlockSpec((tm, d), lambda i: (i, k))
    full = pl.BlockSpec((d, d), lambda i: (0, 0))
    return pl.pallas_call(
        _out_kernel,
        out_shape=jax.ShapeDtypeStruct((m, d), F32),
        grid=(m // tm,),
        in_specs=[row(0), row(0), row(0), row(8), row(9),
                  pl.BlockSpec((1, 1, d), lambda i: (i // per_b, 0, 0)),
                  pl.BlockSpec((1, d), lambda i: (0, 0)),
                  full, full, full],
        out_specs=row(0),
        compiler_params=pltpu.CompilerParams(
            dimension_semantics=("arbitrary",),
            vmem_limit_bytes=VMEM_LIMIT_BYTES),
        name="out_merge",
    )(x2, y_ret, y_diff, proj, proj, gate3, post_g2, wr, wd, wo)


def _rope_freqs():
    half_r = RET_HEAD_DIM // 2
    fr = RET_THETA ** (-jnp.arange(half_r, dtype=F32) * 2.0 / RET_HEAD_DIM)
    half_d = DIFF_ROT_DIM // 2
    fd = ROPE_THETA ** (-jnp.arange(half_d, dtype=F32) * 2.0 / DIFF_ROT_DIM)
    ch = np.arange(LANES) % DIFF_HEAD_DIM
    fd_lane = jnp.where(ch < DIFF_ROT_DIM, fd[ch % half_d], 0.0).astype(F32)
    return fr.reshape(1, half_r), fd_lane.reshape(1, LANES)


def kernel(x, c, positions, ada_w, ada_b, pre_norm_g, w_in, ret_norm_g, diff_lambda_q1, diff_lambda_k1, diff_lambda_q2, diff_lambda_k2, diff_subln_g, w_branch_ret, w_branch_diff, w_out, post_norm_g):
    bsz, seq, d = x.shape
    depth = ada_w.shape[0]
    m = bsz * seq
    freq_r, freq_d = _rope_freqs()
    pos2 = positions.reshape(m, 1)
    x2 = x.reshape(m, d)
    for l in range(depth):
        lambda_init = 0.8 - 0.6 * math.exp(-0.3 * l)
        mod = _ada_mod(c, ada_w[l].astype(BF16), ada_b[l].reshape(1, 3 * d))
        shift3 = mod[:, :d].reshape(bsz, 1, d)
        scale3 = mod[:, d:2 * d].reshape(bsz, 1, d)
        gate3 = mod[:, 2 * d:].reshape(bsz, 1, d)
        proj = _in_proj(x2, pos2, shift3, scale3, pre_norm_g[l].reshape(1, d), freq_r, freq_d,
                        w_in[l].astype(BF16), seq=seq, tm=1024)
        y_ret = _retention(proj, ret_norm_g[l].reshape(1, d), bsz=bsz, seq=seq, tb=256)
        y_diff = _diff_attn(proj,
                            diff_lambda_q1[l].reshape(1, -1), diff_lambda_k1[l].reshape(1, -1),
                            diff_lambda_q2[l].reshape(1, -1), diff_lambda_k2[l].reshape(1, -1),
                            diff_subln_g[l].reshape(1, -1),
                            bsz=bsz, seq=seq, tq=512, tk=512, lambda_init=lambda_init)
        x2 = _out_merge(x2, y_ret, y_diff, proj, gate3, post_norm_g[l].reshape(1, d),
                        w_branch_ret[l].astype(BF16), w_branch_diff[l].astype(BF16),
                        w_out[l].astype(BF16), seq=seq, tm=512)
    return x2.reshape(bsz, seq, d)
```

```python
import functools
import math

import jax
import jax.numpy as jnp
import numpy as np
from jax import lax
from jax.experimental import pallas as pl
from jax.experimental.pallas import tpu as pltpu

D_MODEL = 1024
CHUNK = 64
RMS_EPS = 1e-6
GN_EPS = 1e-5
RET_HEAD_DIM = 256
RET_HEADS = D_MODEL // RET_HEAD_DIM
RET_THETA = 10000.0
DIFF_HEAD_DIM = 64
DIFF_HEADS = D_MODEL // (2 * DIFF_HEAD_DIM)
DIFF_ROT_DIM = DIFF_HEAD_DIM // 4
ROPE_THETA = 500000.0
N_SECTIONS = 10
LANES = 128
LOG2E = math.log2(math.e)

VMEM_LIMIT_BYTES = 56 * 1024 * 1024

BF16 = jnp.bfloat16
F32 = jnp.float32


def _silu(v):
    return v * jax.nn.sigmoid(v)


def _ada_kernel(c_ref, w_ref, b_ref, o_ref):
    sc = _silu(c_ref[...]).astype(BF16)
    o_ref[...] = jnp.dot(sc, w_ref[...], preferred_element_type=F32) + b_ref[...]


def _ada_mod(c, w_bf16, b):
    bsz, d = c.shape
    n = w_bf16.shape[1]
    tn = 1024
    return pl.pallas_call(
        _ada_kernel,
        out_shape=jax.ShapeDtypeStruct((bsz, n), F32),
        grid=(n // tn,),
        in_specs=[
            pl.BlockSpec((bsz, d), lambda j: (0, 0)),
            pl.BlockSpec((d, tn), lambda j: (0, j)),
            pl.BlockSpec((1, tn), lambda j: (0, j)),
        ],
        out_specs=pl.BlockSpec((bsz, tn), lambda j: (0, j)),
        name="ada_mod",
    )(c, w_bf16, b)


def _in_proj_kernel(x_ref, pos_ref, shift_ref, scale_ref, g_ref, fr_ref, fd_ref, w_ref, o_ref):
    x = x_ref[...]
    inv = lax.rsqrt(jnp.mean(x * x, axis=-1, keepdims=True) + RMS_EPS)
    h = (x * inv) * g_ref[...]
    h = (h * (1.0 + scale_ref[0]) + shift_ref[0]).astype(BF16)

    pos = pos_ref[...].astype(F32)
    ang_r = pos * fr_ref[...]
    rc = jnp.cos(ang_r)
    rs = jnp.sin(ang_r)
    ang_d = pos * fd_ref[...]
    ch = lax.broadcasted_iota(jnp.int32, ang_d.shape, 1) % DIFF_HEAD_DIM
    half = DIFF_ROT_DIM // 2
    cd = jnp.cos(ang_d)
    sd = jnp.sin(ang_d)
    dc = jnp.where(ch < DIFF_ROT_DIM, cd, 1.0)
    ds1 = jnp.where(ch < half, -sd, 0.0)
    ds2 = jnp.where((ch >= half) & (ch < DIFF_ROT_DIM), sd, 0.0)

    def section(j):
        lo = j * D_MODEL
        return lo, jnp.dot(h, w_ref[:, lo:lo + D_MODEL], preferred_element_type=F32)

    def ret_rope(j, mult):
        base, acc = section(j)
        hw = RET_HEAD_DIM // 2
        for hd in range(RET_HEADS):
            lo = hd * RET_HEAD_DIM
            x1 = acc[:, lo:lo + hw]
            x2 = acc[:, lo + hw:lo + 2 * hw]
            o_ref[:, base + lo:base + lo + hw] = ((x1 * rc - x2 * rs) * mult).astype(BF16)
            o_ref[:, base + lo + hw:base + lo + 2 * hw] = ((x2 * rc + x1 * rs) * mult).astype(BF16)

    def diff_rope(j, mult):
        base, acc = section(j)
        for sl in range(D_MODEL // LANES):
            xs = acc[:, sl * LANES:(sl + 1) * LANES]
            up = pltpu.roll(xs, LANES - half, axis=1)
            dn = pltpu.roll(xs, half, axis=1)
            o_ref[:, base + sl * LANES:base + (sl + 1) * LANES] = (
                (xs * dc + up * ds1 + dn * ds2) * mult).astype(BF16)

    def plain(j):
        base, acc = section(j)
        o_ref[:, base:base + D_MODEL] = acc.astype(BF16)

    ret_rope(0, 1.0)
    ret_rope(1, RET_HEAD_DIM ** -0.5)
    plain(2)
    plain(3)
    diff_rope(4, DIFF_HEAD_DIM ** -0.5 * LOG2E)
    diff_rope(5, 1.0)
    for j in range(6, N_SECTIONS):
        plain(j)


def _in_proj(x2, pos2, shift3, scale3, g2, freq_r, freq_d, w_bf16, *, seq, tm):
    m, d = x2.shape
    n = w_bf16.shape[1]
    per_b = seq // tm
    return pl.pallas_call(
        _in_proj_kernel,
        out_shape=jax.ShapeDtypeStruct((m, n), BF16),
        grid=(m // tm,),
        in_specs=[
            pl.BlockSpec((tm, d), lambda i: (i, 0)),
            pl.BlockSpec((tm, 1), lambda i: (i, 0)),
            pl.BlockSpec((1, 1, d), lambda i: (i // per_b, 0, 0)),
            pl.BlockSpec((1, 1, d), lambda i: (i // per_b, 0, 0)),
            pl.BlockSpec((1, d), lambda i: (0, 0)),
            pl.BlockSpec((1, LANES), lambda i: (0, 0)),
            pl.BlockSpec((1, LANES), lambda i: (0, 0)),
            pl.BlockSpec((d, n), lambda i: (0, 0), pipeline_mode=pl.Buffered(1)),
        ],
        out_specs=pl.BlockSpec((tm, n), lambda i: (i, 0)),
        compiler_params=pltpu.CompilerParams(
            dimension_semantics=("arbitrary",),
            vmem_limit_bytes=VMEM_LIMIT_BYTES),
        name="in_proj",
    )(x2, pos2, shift3, scale3, g2, freq_r, freq_d, w_bf16)


def _log_gammas():
    lo, hi = math.log(1.0 / 32), math.log(1.0 / 512)
    return [math.log1p(-math.exp(lo + (hi - lo) * i / (RET_HEADS - 1))) for i in range(RET_HEADS)]


def _retention_kernel(q_ref, k_ref, v_ref, g_ref, ng_ref, o_ref, state_ref, *, tb):
    n = pl.program_id(1)

    @pl.when(n == 0)
    def _reset():
        state_ref[...] = jnp.zeros_like(state_ref)

    row = lax.broadcasted_iota(jnp.int32, (tb, tb), 0)
    col = lax.broadcasted_iota(jnp.int32, (tb, tb), 1)
    dist = jnp.abs(row - col).astype(F32)
    visible = (col // CHUNK) <= (row // CHUNK)
    t_col = lax.broadcasted_iota(jnp.int32, (tb, 1), 0).astype(F32)

    for hd, lg in enumerate(_log_gammas()):
        lo = hd * RET_HEAD_DIM
        q = q_ref[:, lo:lo + RET_HEAD_DIM]
        k = k_ref[:, lo:lo + RET_HEAD_DIM]
        v = v_ref[:, lo:lo + RET_HEAD_DIM]
        decay = jnp.where(visible, jnp.exp(dist * lg), 0.0)
        scores = lax.dot_general(q, k, (((1,), (1,)), ((), ())), preferred_element_type=F32)
        scores = (scores * decay).astype(BF16)
        intra = jnp.dot(scores, v, preferred_element_type=F32)
        state = state_ref[hd]
        xi = jnp.exp((t_col + 1.0) * lg)
        inter = jnp.dot(q, state.astype(BF16), preferred_element_type=F32) * xi
        zeta = jnp.exp((tb - 1.0 - t_col) * lg)
        kz = (k.astype(F32) * zeta).astype(BF16)
        upd = lax.dot_general(kz, v, (((0,), (0,)), ((), ())), preferred_element_type=F32)
        state_ref[hd] = state * math.exp(tb * lg) + upd

        o = intra + inter
        mu = jnp.mean(o, axis=-1, keepdims=True)
        oc = o - mu
        var = jnp.mean(oc * oc, axis=-1, keepdims=True)
        o = oc * lax.rsqrt(var + GN_EPS) * ng_ref[:, lo:lo + RET_HEAD_DIM]
        gate = g_ref[:, lo:lo + RET_HEAD_DIM].astype(F32)
        o_ref[:, lo:lo + RET_HEAD_DIM] = (o * _silu(gate)).astype(BF16)


def _retention(proj, ret_norm_g2, *, bsz, seq, tb):
    m = proj.shape[0]
    nb = seq // tb
    d = D_MODEL

    def sec(k):
        return pl.BlockSpec((tb, d), lambda b, n: (b * nb + n, k))

    return pl.pallas_call(
        functools.partial(_retention_kernel, tb=tb),
        out_shape=jax.ShapeDtypeStruct((m, d), BF16),
        grid=(bsz, nb),
        in_specs=[sec(0), sec(1), sec(2), sec(3), pl.BlockSpec((1, d), lambda b, n: (0, 0))],
        out_specs=pl.BlockSpec((tb, d), lambda b, n: (b * nb + n, 0)),
        scratch_shapes=[pltpu.VMEM((RET_HEADS, RET_HEAD_DIM, RET_HEAD_DIM), F32)],
        compiler_params=pltpu.CompilerParams(
            dimension_semantics=("arbitrary", "arbitrary"),
            vmem_limit_bytes=VMEM_LIMIT_BYTES),
        name="retention",
    )(proj, proj, proj, proj, ret_norm_g2)


def _diff_attn_kernel(q_ref, k_ref, v_ref, g_ref, lq1_ref, lk1_ref, lq2_ref, lk2_ref, sg_ref,
                      o_ref, acc_ref, s_ref, *, seq, tile, n_buf, lambda_init):
    hd = DIFF_HEAD_DIM
    n_tiles = seq // tile

    lam = (jnp.exp(jnp.sum(lq1_ref[...] * lk1_ref[...], axis=-1, keepdims=True))
           - jnp.exp(jnp.sum(lq2_ref[...] * lk2_ref[...], axis=-1, keepdims=True))
           + lambda_init)

    items = [(i, j) for i in range(n_tiles) for j in [i] + list(range(i))]
    row_chunk = lax.broadcasted_iota(jnp.int32, (tile, tile), 0) // CHUNK
    col_chunk = lax.broadcasted_iota(jnp.int32, (tile, tile), 1) // CHUNK
    visible = row_chunk <= col_chunk
    lane = lax.broadcasted_iota(jnp.int32, (tile, 2 * hd), 1)

    masked_q = {}

    def q_maps(i):
        if i not in masked_q:
            q = q_ref[i * tile:(i + 1) * tile, :]
            zero = jnp.zeros_like(q)
            masked_q.clear()
            masked_q[i] = (jnp.where(lane < hd, q, zero), jnp.where(lane >= hd, q, zero))
        return masked_q[i]

    def scores(t):
        i, j = items[t]
        k = k_ref[j * tile:(j + 1) * tile, :]
        mx = []
        for mp in range(2):
            s = lax.dot_general(k, q_maps(i)[mp], (((1,), (1,)), ((), ())),
                                preferred_element_type=F32)
            if i == j:
                s = jnp.where(visible, s, -jnp.inf)
            s_ref[t % n_buf, mp] = s
            mx.append(jnp.max(s, axis=0, keepdims=True))
        return mx

    def consume(t, mx, state):
        i, j = items[t]
        v = v_ref[j * tile:(j + 1) * tile, :]
        new_state = []
        for mp in range(2):
            if i == j:
                m_new = mx[mp]
            else:
                m_old, l_old = state[mp]
                m_new = jnp.maximum(m_old, mx[mp])
                alpha = jnp.exp2(m_old - m_new)
            p = jnp.exp2(s_ref[t % n_buf, mp] - m_new)
            l_new = jnp.sum(p, axis=0, keepdims=True)
            pv = lax.dot_general(v, p.astype(BF16), (((0,), (0,)), ((), ())),
                                 preferred_element_type=F32)
            if i == j:
                acc_ref[mp] = pv
            else:
                l_new = alpha * l_old + l_new
                acc_ref[mp] = alpha * acc_ref[mp] + pv
            new_state.append((m_new, l_new))
        return new_state

    def finalize(i, state):
        o_t = acc_ref[0] / state[0][1] - lam * (acc_ref[1] / state[1][1])
        o = o_t.T
        o = o * lax.rsqrt(jnp.mean(o * o, axis=-1, keepdims=True) + RMS_EPS)
        o = o * sg_ref[...] * (1.0 - lambda_init)
        gate = g_ref[i * tile:(i + 1) * tile, :].astype(F32)
        o_ref[i * tile:(i + 1) * tile, :] = (o * _silu(gate)).astype(BF16)

    pending = scores(0)
    state = None
    for t in range(len(items)):
        nxt = scores(t + 1) if t + 1 < len(items) else None
        state = consume(t, pending, state)
        i, j = items[t]
        if t + 1 == len(items) or items[t + 1][0] != i:
            finalize(i, state)
        pending = nxt


def _diff_attn(proj, lq1, lk1, lq2, lk2, subln_g2, *, bsz, seq, tile, n_buf, lambda_init):
    m = proj.shape[0]
    w = 2 * DIFF_HEAD_DIM
    per_sec = D_MODEL // w

    def head_cols(sec_idx):
        return pl.BlockSpec((seq, w), lambda b, h: (b, sec_idx * per_sec + h))

    small = pl.BlockSpec((1, DIFF_HEAD_DIM), lambda b, h: (0, 0))
    return pl.pallas_call(
        functools.partial(_diff_attn_kernel, seq=seq, tile=tile, n_buf=n_buf,
                          lambda_init=lambda_init),
        out_shape=jax.ShapeDtypeStruct((m, D_MODEL), BF16),
        grid=(bsz, DIFF_HEADS),
        in_specs=[head_cols(4), head_cols(5), head_cols(6), head_cols(7),
                  small, small, small, small, pl.BlockSpec((1, w), lambda b, h: (0, 0))],
        out_specs=pl.BlockSpec((seq, w), lambda b, h: (b, h)),
        scratch_shapes=[pltpu.VMEM((2, w, tile), F32), pltpu.VMEM((n_buf, 2, tile, tile), F32)],
        compiler_params=pltpu.CompilerParams(
            dimension_semantics=("arbitrary", "arbitrary"),
            vmem_limit_bytes=VMEM_LIMIT_BYTES),
        name="diff_attn",
    )(proj, proj, proj, proj, lq1, lk1, lq2, lk2, subln_g2)


def _out_kernel(x_ref, yr_ref, yd_ref, mr_ref, md_ref, gate_ref, pg_ref, wr_ref, wd_ref, wo_ref, o_ref):
    br = jnp.dot(yr_ref[...], wr_ref[...], preferred_element_type=F32)
    bd = jnp.dot(yd_ref[...], wd_ref[...], preferred_element_type=F32)
    merged = (jax.nn.sigmoid(mr_ref[...].astype(F32)) * br
              + jax.nn.sigmoid(md_ref[...].astype(F32)) * bd)
    out = jnp.dot(merged.astype(BF16), wo_ref[...], preferred_element_type=F32)
    normed = out * lax.rsqrt(jnp.mean(out * out, axis=-1, keepdims=True) + RMS_EPS) * pg_ref[...]
    o_ref[...] = x_ref[...] + gate_ref[0] * normed


def _out_merge(x2, y_ret, y_diff, proj, gate3, post_g2, wr, wd, wo, *, seq, tm):
    m, d = x2.shape
    per_b = seq // tm
    row = lambda k: pl.BlockSpec((tm, d), lambda i: (i, k))
    full = pl.BlockSpec((d, d), lambda i: (0, 0))
    return pl.pallas_call(
        _out_kernel,
        out_shape=jax.ShapeDtypeStruct((m, d), F32),
        grid=(m // tm,),
        in_specs=[row(0), row(0), row(0), row(8), row(9),
                  pl.BlockSpec((1, 1, d), lambda i: (i // per_b, 0, 0)),
                  pl.BlockSpec((1, d), lambda i: (0, 0)),
                  full, full, full],
        out_specs=row(0),
        compiler_params=pltpu.CompilerParams(
            dimension_semantics=("arbitrary",),
            vmem_limit_bytes=VMEM_LIMIT_BYTES),
        name="out_merge",
    )(x2, y_ret, y_diff, proj, proj, gate3, post_g2, wr, wd, wo)


def _rope_freqs():
    half_r = RET_HEAD_DIM // 2
    fr = RET_THETA ** (-jnp.arange(half_r, dtype=F32) * 2.0 / RET_HEAD_DIM)
    half_d = DIFF_ROT_DIM // 2
    fd = ROPE_THETA ** (-jnp.arange(half_d, dtype=F32) * 2.0 / DIFF_ROT_DIM)
    ch = np.arange(LANES) % DIFF_HEAD_DIM
    fd_lane = jnp.where(ch < DIFF_ROT_DIM, fd[ch % half_d], 0.0).astype(F32)
    return fr.reshape(1, half_r), fd_lane.reshape(1, LANES)


def kernel(x, c, positions, ada_w, ada_b, pre_norm_g, w_in, ret_norm_g, diff_lambda_q1, diff_lambda_k1, diff_lambda_q2, diff_lambda_k2, diff_subln_g, w_branch_ret, w_branch_diff, w_out, post_norm_g):
    bsz, seq, d = x.shape
    depth = ada_w.shape[0]
    m = bsz * seq
    freq_r, freq_d = _rope_freqs()
    pos2 = positions.reshape(m, 1)
    x2 = x.reshape(m, d)
    for l in range(depth):
        lambda_init = 0.8 - 0.6 * math.exp(-0.3 * l)
        mod = _ada_mod(c, ada_w[l].astype(BF16), ada_b[l].reshape(1, 3 * d))
        shift3 = mod[:, :d].reshape(bsz, 1, d)
        scale3 = mod[:, d:2 * d].reshape(bsz, 1, d)
        gate3 = mod[:, 2 * d:].reshape(bsz, 1, d)
        proj = _in_proj(x2, pos2, shift3, scale3, pre_norm_g[l].reshape(1, d), freq_r, freq_d,
                        w_in[l].astype(BF16), seq=seq, tm=512)
        y_ret = _retention(proj, ret_norm_g[l].reshape(1, d), bsz=bsz, seq=seq, tb=256)
        y_diff = _diff_attn(proj,
                            diff_lambda_q1[l].reshape(1, -1), diff_lambda_k1[l].reshape(1, -1),
                            diff_lambda_q2[l].reshape(1, -1), diff_lambda_k2[l].reshape(1, -1),
                            diff_subln_g[l].reshape(1, -1),
                            bsz=bsz, seq=seq, tile=512, n_buf=2, lambda_init=lambda_init)
        x2 = _out_merge(x2, y_ret, y_diff, proj, gate3, post_norm_g[l].reshape(1, d),
                        w_branch_ret[l].astype(BF16), w_branch_diff[l].astype(BF16),
                        w_out[l].astype(BF16), seq=seq, tm=512)
    return x2.reshape(bsz, seq, d)
```

```python
import functools
import math

import jax
import jax.numpy as jnp
import numpy as np
from jax import lax
from jax.experimental import pallas as pl
from jax.experimental.pallas import tpu as pltpu

D_MODEL = 1024
CHUNK = 64
RMS_EPS = 1e-6
GN_EPS = 1e-5
RET_HEAD_DIM = 256
RET_HEADS = D_MODEL // RET_HEAD_DIM
RET_THETA = 10000.0
DIFF_HEAD_DIM = 64
DIFF_HEADS = D_MODEL // (2 * DIFF_HEAD_DIM)
DIFF_ROT_DIM = DIFF_HEAD_DIM // 4
ROPE_THETA = 500000.0
N_SECTIONS = 10
LANES = 128
LOG2E = math.log2(math.e)

VMEM_LIMIT_BYTES = 56 * 1024 * 1024

BF16 = jnp.bfloat16
F32 = jnp.float32


def _silu(v):
    return v * jax.nn.sigmoid(v)


def _ada_kernel(c_ref, w_ref, b_ref, o_ref):
    sc = _silu(c_ref[...]).astype(BF16)
    o_ref[...] = jnp.dot(sc, w_ref[...], preferred_element_type=F32) + b_ref[...]


def _ada_mod(c, w_bf16, b):
    bsz, d = c.shape
    n = w_bf16.shape[1]
    tn = 1024
    return pl.pallas_call(
        _ada_kernel,
        out_shape=jax.ShapeDtypeStruct((bsz, n), F32),
        grid=(n // tn,),
        in_specs=[
            pl.BlockSpec((bsz, d), lambda j: (0, 0)),
            pl.BlockSpec((d, tn), lambda j: (0, j)),
            pl.BlockSpec((1, tn), lambda j: (0, j)),
        ],
        out_specs=pl.BlockSpec((bsz, tn), lambda j: (0, j)),
        name="ada_mod",
    )(c, w_bf16, b)


def _in_proj_kernel(x_ref, pos_ref, shift_ref, scale_ref, g_ref, fr_ref, fd_ref, w_ref, o_ref):
    x = x_ref[...]
    inv = lax.rsqrt(jnp.mean(x * x, axis=-1, keepdims=True) + RMS_EPS)
    h = (x * inv) * g_ref[...]
    h = (h * (1.0 + scale_ref[0]) + shift_ref[0]).astype(BF16)

    pos = pos_ref[...].astype(F32)
    ang_r = pos * fr_ref[...]
    rc = jnp.cos(ang_r)
    rs = jnp.sin(ang_r)
    ang_d = pos * fd_ref[...]
    ch = lax.broadcasted_iota(jnp.int32, ang_d.shape, 1) % DIFF_HEAD_DIM
    half = DIFF_ROT_DIM // 2
    cd = jnp.cos(ang_d)
    sd = jnp.sin(ang_d)
    dc = jnp.where(ch < DIFF_ROT_DIM, cd, 1.0)
    ds1 = jnp.where(ch < half, -sd, 0.0)
    ds2 = jnp.where((ch >= half) & (ch < DIFF_ROT_DIM), sd, 0.0)

    def section(j):
        lo = j * D_MODEL
        return lo, jnp.dot(h, w_ref[:, lo:lo + D_MODEL], preferred_element_type=F32)

    def ret_rope(j, mult):
        base, acc = section(j)
        hw = RET_HEAD_DIM // 2
        for hd in range(RET_HEADS):
            lo = hd * RET_HEAD_DIM
            x1 = acc[:, lo:lo + hw]
            x2 = acc[:, lo + hw:lo + 2 * hw]
            o_ref[:, base + lo:base + lo + hw] = ((x1 * rc - x2 * rs) * mult).astype(BF16)
            o_ref[:, base + lo + hw:base + lo + 2 * hw] = ((x2 * rc + x1 * rs) * mult).astype(BF16)

    def diff_rope(j, mult):
        base, acc = section(j)
        for sl in range(D_MODEL // LANES):
            xs = acc[:, sl * LANES:(sl + 1) * LANES]
            up = pltpu.roll(xs, LANES - half, axis=1)
            dn = pltpu.roll(xs, half, axis=1)
            o_ref[:, base + sl * LANES:base + (sl + 1) * LANES] = (
                (xs * dc + up * ds1 + dn * ds2) * mult).astype(BF16)

    def plain(j):
        base, acc = section(j)
        o_ref[:, base:base + D_MODEL] = acc.astype(BF16)

    ret_rope(0, 1.0)
    ret_rope(1, RET_HEAD_DIM ** -0.5)
    plain(2)
    plain(3)
    diff_rope(4, DIFF_HEAD_DIM ** -0.5 * LOG2E)
    diff_rope(5, 1.0)
    for j in range(6, N_SECTIONS):
        plain(j)


def _in_proj(x2, pos2, shift3, scale3, g2, freq_r, freq_d, w_bf16, *, seq, tm):
    m, d = x2.shape
    n = w_bf16.shape[1]
    per_b = seq // tm
    return pl.pallas_call(
        _in_proj_kernel,
        out_shape=jax.ShapeDtypeStruct((m, n), BF16),
        grid=(m // tm,),
        in_specs=[
            pl.BlockSpec((tm, d), lambda i: (i, 0)),
            pl.BlockSpec((tm, 1), lambda i: (i, 0)),
            pl.BlockSpec((1, 1, d), lambda i: (i // per_b, 0, 0)),
            pl.BlockSpec((1, 1, d), lambda i: (i // per_b, 0, 0)),
            pl.BlockSpec((1, d), lambda i: (0, 0)),
            pl.BlockSpec((1, LANES), lambda i: (0, 0)),
            pl.BlockSpec((1, LANES), lambda i: (0, 0)),
            pl.BlockSpec((d, n), lambda i: (0, 0), pipeline_mode=pl.Buffered(1)),
        ],
        out_specs=pl.BlockSpec((tm, n), lambda i: (i, 0)),
        compiler_params=pltpu.CompilerParams(
            dimension_semantics=("arbitrary",),
            vmem_limit_bytes=VMEM_LIMIT_BYTES),
        name="in_proj",
    )(x2, pos2, shift3, scale3, g2, freq_r, freq_d, w_bf16)


def _log_gammas():
    lo, hi = math.log(1.0 / 32), math.log(1.0 / 512)
    return [math.log1p(-math.exp(lo + (hi - lo) * i / (RET_HEADS - 1))) for i in range(RET_HEADS)]


def _retention_kernel(q_ref, k_ref, v_ref, g_ref, ng_ref, o_ref,
                      state_ref, decay_ref, xi_ref, zeta_ref, *, tb):
    n = pl.program_id(1)

    @pl.when((pl.program_id(0) == 0) & (n == 0))
    def _tables():
        row = lax.broadcasted_iota(jnp.int32, (tb, tb), 0)
        col = lax.broadcasted_iota(jnp.int32, (tb, tb), 1)
        dist = jnp.abs(row - col).astype(F32)
        visible = (col // CHUNK) <= (row // CHUNK)
        t = lax.broadcasted_iota(jnp.int32, (tb, RET_HEAD_DIM), 0).astype(F32)
        for hd, lg in enumerate(_log_gammas()):
            decay_ref[hd] = jnp.where(visible, jnp.exp(dist * lg), 0.0)
            xi_ref[hd] = jnp.exp((t + 1.0) * lg).astype(BF16)
            zeta_ref[hd] = jnp.exp((tb - 1.0 - t) * lg).astype(BF16)

    @pl.when(n == 0)
    def _reset():
        state_ref[...] = jnp.zeros_like(state_ref)

    for hd, lg in enumerate(_log_gammas()):
        lo = hd * RET_HEAD_DIM
        q = q_ref[:, lo:lo + RET_HEAD_DIM]
        k = k_ref[:, lo:lo + RET_HEAD_DIM]
        v = v_ref[:, lo:lo + RET_HEAD_DIM]
        scores = lax.dot_general(q, k, (((1,), (1,)), ((), ())), preferred_element_type=F32)
        scores = (scores * decay_ref[hd]).astype(BF16)
        state = state_ref[hd]
        o = (jnp.dot(scores, v, preferred_element_type=F32)
             + jnp.dot(q * xi_ref[hd], state.astype(BF16), preferred_element_type=F32))
        upd = lax.dot_general(k * zeta_ref[hd], v, (((0,), (0,)), ((), ())),
                              preferred_element_type=F32)
        state_ref[hd] = state * math.exp(tb * lg) + upd

        mu = jnp.mean(o, axis=-1, keepdims=True)
        oc = o - mu
        var = jnp.mean(oc * oc, axis=-1, keepdims=True)
        o = oc * lax.rsqrt(var + GN_EPS) * ng_ref[:, lo:lo + RET_HEAD_DIM]
        gate = g_ref[:, lo:lo + RET_HEAD_DIM].astype(F32)
        o_ref[:, lo:lo + RET_HEAD_DIM] = (o * _silu(gate)).astype(BF16)


def _retention(proj, ret_norm_g2, *, bsz, seq, tb):
    m = proj.shape[0]
    nb = seq // tb
    d = D_MODEL

    def sec(k):
        return pl.BlockSpec((tb, d), lambda b, n: (b * nb + n, k))

    return pl.pallas_call(
        functools.partial(_retention_kernel, tb=tb),
        out_shape=jax.ShapeDtypeStruct((m, d), BF16),
        grid=(bsz, nb),
        in_specs=[sec(0), sec(1), sec(2), sec(3), pl.BlockSpec((1, d), lambda b, n: (0, 0))],
        out_specs=pl.BlockSpec((tb, d), lambda b, n: (b * nb + n, 0)),
        scratch_shapes=[pltpu.VMEM((RET_HEADS, RET_HEAD_DIM, RET_HEAD_DIM), F32),
                        pltpu.VMEM((RET_HEADS, tb, tb), F32),
                        pltpu.VMEM((RET_HEADS, tb, RET_HEAD_DIM), BF16),
                        pltpu.VMEM((RET_HEADS, tb, RET_HEAD_DIM), BF16)],
        compiler_params=pltpu.CompilerParams(
            dimension_semantics=("arbitrary", "arbitrary"),
            vmem_limit_bytes=VMEM_LIMIT_BYTES),
        name="retention",
    )(proj, proj, proj, proj, ret_norm_g2)


def _diff_attn_kernel(q_ref, k_ref, v_ref, g_ref, lq1_ref, lk1_ref, lq2_ref, lk2_ref, sg_ref,
                      o_ref, acc_ref, s_ref, *, seq, tile, n_buf, lambda_init):
    hd = DIFF_HEAD_DIM
    n_tiles = seq // tile

    lam = (jnp.exp(jnp.sum(lq1_ref[...] * lk1_ref[...], axis=-1, keepdims=True))
           - jnp.exp(jnp.sum(lq2_ref[...] * lk2_ref[...], axis=-1, keepdims=True))
           + lambda_init)

    items = [(i, j) for i in range(n_tiles) for j in [i] + list(range(i))]
    row_chunk = lax.broadcasted_iota(jnp.int32, (tile, tile), 0) // CHUNK
    col_chunk = lax.broadcasted_iota(jnp.int32, (tile, tile), 1) // CHUNK
    visible = row_chunk <= col_chunk
    lane = lax.broadcasted_iota(jnp.int32, (tile, 2 * hd), 1)

    masked_q = {}

    def q_maps(i):
        if i not in masked_q:
            q = q_ref[i * tile:(i + 1) * tile, :]
            zero = jnp.zeros_like(q)
            masked_q.clear()
            masked_q[i] = (jnp.where(lane < hd, q, zero), jnp.where(lane >= hd, q, zero))
        return masked_q[i]

    def scores(t):
        i, j = items[t]
        k = k_ref[j * tile:(j + 1) * tile, :]
        mx = []
        for mp in range(2):
            s = lax.dot_general(k, q_maps(i)[mp], (((1,), (1,)), ((), ())),
                                preferred_element_type=F32)
            if i == j:
                s = jnp.where(visible, s, -jnp.inf)
            s_ref[t % n_buf, mp] = s
            mx.append(jnp.max(s, axis=0, keepdims=True))
        return mx

    def consume(t, mx, state):
        i, j = items[t]
        v = v_ref[j * tile:(j + 1) * tile, :]
        new_state = []
        for mp in range(2):
            if i == j:
                m_new = mx[mp]
            else:
                m_old, l_old = state[mp]
                m_new = jnp.maximum(m_old, mx[mp])
                alpha = jnp.exp2(m_old - m_new)
            p = jnp.exp2(s_ref[t % n_buf, mp] - m_new)
            l_new = jnp.sum(p, axis=0, keepdims=True)
            pv = lax.dot_general(v, p.astype(BF16), (((0,), (0,)), ((), ())),
                                 preferred_element_type=F32)
            if i == j:
                acc_ref[mp] = pv
            else:
                l_new = alpha * l_old + l_new
                acc_ref[mp] = alpha * acc_ref[mp] + pv
            new_state.append((m_new, l_new))
        return new_state

    def finalize(i, state):
        o_t = acc_ref[0] / state[0][1] - lam * (acc_ref[1] / state[1][1])
        o = o_t.T
        o = o * lax.rsqrt(jnp.mean(o * o, axis=-1, keepdims=True) + RMS_EPS)
        o = o * sg_ref[...] * (1.0 - lambda_init)
        gate = g_ref[i * tile:(i + 1) * tile, :].astype(F32)
        o_ref[i * tile:(i + 1) * tile, :] = (o * _silu(gate)).astype(BF16)

    ahead = n_buf - 1
    pending = [scores(t) for t in range(min(ahead, len(items)))]
    state = None
    for t in range(len(items)):
        if t + ahead < len(items):
            pending.append(scores(t + ahead))
        state = consume(t, pending.pop(0), state)
        i, j = items[t]
        if t + 1 == len(items) or items[t + 1][0] != i:
            finalize(i, state)


def _diff_attn(proj, lq1, lk1, lq2, lk2, subln_g2, *, bsz, seq, tile, n_buf, lambda_init):
    m = proj.shape[0]
    w = 2 * DIFF_HEAD_DIM
    per_sec = D_MODEL // w

    def head_cols(sec_idx):
        return pl.BlockSpec((seq, w), lambda b, h: (b, sec_idx * per_sec + h))

    small = pl.BlockSpec((1, DIFF_HEAD_DIM), lambda b, h: (0, 0))
    return pl.pallas_call(
        functools.partial(_diff_attn_kernel, seq=seq, tile=tile, n_buf=n_buf,
                          lambda_init=lambda_init),
        out_shape=jax.ShapeDtypeStruct((m, D_MODEL), BF16),
        grid=(bsz, DIFF_HEADS),
        in_specs=[head_cols(4), head_cols(5), head_cols(6), head_cols(7),
                  small, small, small, small, pl.BlockSpec((1, w), lambda b, h: (0, 0))],
        out_specs=pl.BlockSpec((seq, w), lambda b, h: (b, h)),
        scratch_shapes=[pltpu.VMEM((2, w, tile), F32), pltpu.VMEM((n_buf, 2, tile, tile), F32)],
        compiler_params=pltpu.CompilerParams(
            dimension_semantics=("arbitrary", "arbitrary"),
            vmem_limit_bytes=VMEM_LIMIT_BYTES),
        name="diff_attn",
    )(proj, proj, proj, proj, lq1, lk1, lq2, lk2, subln_g2)


def _out_kernel(x_ref, yr_ref, yd_ref, mr_ref, md_ref, gate_ref, pg_ref, wr_ref, wd_ref, wo_ref, o_ref):
    br = jnp.dot(yr_ref[...], wr_ref[...], preferred_element_type=F32)
    bd = jnp.dot(yd_ref[...], wd_ref[...], preferred_element_type=F32)
    merged = (jax.nn.sigmoid(mr_ref[...].astype(F32)) * br
              + jax.nn.sigmoid(md_ref[...].astype(F32)) * bd)
    out = jnp.dot(merged.astype(BF16), wo_ref[...], preferred_element_type=F32)
    normed = out * lax.rsqrt(jnp.mean(out * out, axis=-1, keepdims=True) + RMS_EPS) * pg_ref[...]
    o_ref[...] = x_ref[...] + gate_ref[0] * normed


def _out_merge(x2, y_ret, y_diff, proj, gate3, post_g2, wr, wd, wo, *, seq, tm):
    m, d = x2.shape
    per_b = seq // tm
    row = lambda k: pl.BlockSpec((tm, d), lambda i: (i, k))
    full = pl.BlockSpec((d, d), lambda i: (0, 0))
    return pl.pallas_call(
        _out_kernel,
        out_shape=jax.ShapeDtypeStruct((m, d), F32),
        grid=(m // tm,),
        in_specs=[row(0), row(0), row(0), row(8), row(9),
                  pl.BlockSpec((1, 1, d), lambda i: (i // per_b, 0, 0)),
                  pl.BlockSpec((1, d), lambda i: (0, 0)),
                  full, full, full],
        out_specs=row(0),
        compiler_params=pltpu.CompilerParams(
            dimension_semantics=("arbitrary",),
            vmem_limit_bytes=VMEM_LIMIT_BYTES),
        name="out_merge",
    )(x2, y_ret, y_diff, proj, proj, gate3, post_g2, wr, wd, wo)


def _rope_freqs():
    half_r = RET_HEAD_DIM // 2
    fr = RET_THETA ** (-jnp.arange(half_r, dtype=F32) * 2.0 / RET_HEAD_DIM)
    half_d = DIFF_ROT_DIM // 2
    fd = ROPE_THETA ** (-jnp.arange(half_d, dtype=F32) * 2.0 / DIFF_ROT_DIM)
    ch = np.arange(LANES) % DIFF_HEAD_DIM
    fd_lane = jnp.where(ch < DIFF_ROT_DIM, fd[ch % half_d], 0.0).astype(F32)
    return fr.reshape(1, half_r), fd_lane.reshape(1, LANES)


def kernel(x, c, positions, ada_w, ada_b, pre_norm_g, w_in, ret_norm_g, diff_lambda_q1, diff_lambda_k1, diff_lambda_q2, diff_lambda_k2, diff_subln_g, w_branch_ret, w_branch_diff, w_out, post_norm_g):
    bsz, seq, d = x.shape
    depth = ada_w.shape[0]
    m = bsz * seq
    freq_r, freq_d = _rope_freqs()
    pos2 = positions.reshape(m, 1)
    x2 = x.reshape(m, d)
    for l in range(depth):
        lambda_init = 0.8 - 0.6 * math.exp(-0.3 * l)
        mod = _ada_mod(c, ada_w[l].astype(BF16), ada_b[l].reshape(1, 3 * d))
        shift3 = mod[:, :d].reshape(bsz, 1, d)
        scale3 = mod[:, d:2 * d].reshape(bsz, 1, d)
        gate3 = mod[:, 2 * d:].reshape(bsz, 1, d)
        proj = _in_proj(x2, pos2, shift3, scale3, pre_norm_g[l].reshape(1, d), freq_r, freq_d,
                        w_in[l].astype(BF16), seq=seq, tm=512)
        y_ret = _retention(proj, ret_norm_g[l].reshape(1, d), bsz=bsz, seq=seq, tb=512)
        y_diff = _diff_attn(proj,
                            diff_lambda_q1[l].reshape(1, -1), diff_lambda_k1[l].reshape(1, -1),
                            diff_lambda_q2[l].reshape(1, -1), diff_lambda_k2[l].reshape(1, -1),
                            diff_subln_g[l].reshape(1, -1),
                            bsz=bsz, seq=seq, tile=512, n_buf=3, lambda_init=lambda_init)
        x2 = _out_merge(x2, y_ret, y_diff, proj, gate3, post_norm_g[l].reshape(1, d),
                        w_branch_ret[l].astype(BF16), w_branch_diff[l].astype(BF16),
                        w_out[l].astype(BF16), seq=seq, tm=512)
    return x2.reshape(bsz, seq, d)
```

```python
import functools
import math

import jax
import jax.numpy as jnp
import numpy as np
from jax import lax
from jax.experimental import pallas as pl
from jax.experimental.pallas import tpu as pltpu

D_MODEL = 1024
CHUNK = 64
RMS_EPS = 1e-6
GN_EPS = 1e-5
RET_HEAD_DIM = 256
RET_HEADS = D_MODEL // RET_HEAD_DIM
RET_THETA = 10000.0
DIFF_HEAD_DIM = 64
DIFF_HEADS = D_MODEL // (2 * DIFF_HEAD_DIM)
DIFF_ROT_DIM = DIFF_HEAD_DIM // 4
ROPE_THETA = 500000.0
N_SECTIONS = 10
LANES = 128
LOG2E = math.log2(math.e)

VMEM_LIMIT_BYTES = 56 * 1024 * 1024

BF16 = jnp.bfloat16
F32 = jnp.float32


def _silu(v):
    return v * jax.nn.sigmoid(v)


def _ada_kernel(c_ref, w_ref, b_ref, o_ref):
    sc = _silu(c_ref[...]).astype(BF16)
    o_ref[...] = jnp.dot(sc, w_ref[...], preferred_element_type=F32) + b_ref[...]


def _ada_mod(c, w_bf16, b):
    bsz, d = c.shape
    n = w_bf16.shape[1]
    tn = 1024
    return pl.pallas_call(
        _ada_kernel,
        out_shape=jax.ShapeDtypeStruct((bsz, n), F32),
        grid=(n // tn,),
        in_specs=[
            pl.BlockSpec((bsz, d), lambda j: (0, 0)),
            pl.BlockSpec((d, tn), lambda j: (0, j)),
            pl.BlockSpec((1, tn), lambda j: (0, j)),
        ],
        out_specs=pl.BlockSpec((bsz, tn), lambda j: (0, j)),
        name="ada_mod",
    )(c, w_bf16, b)


def _in_proj_kernel(x_ref, pos_ref, shift_ref, scale_ref, g_ref, fr_ref, fd_ref, w_ref, o_ref):
    x = x_ref[...]
    inv = lax.rsqrt(jnp.mean(x * x, axis=-1, keepdims=True) + RMS_EPS)
    h = (x * inv) * g_ref[...]
    h = (h * (1.0 + scale_ref[0]) + shift_ref[0]).astype(BF16)

    pos = pos_ref[...].astype(F32)
    ang_r = pos * fr_ref[...]
    rc = jnp.cos(ang_r)
    rs = jnp.sin(ang_r)
    ang_d = pos * fd_ref[...]
    ch = lax.broadcasted_iota(jnp.int32, ang_d.shape, 1) % DIFF_HEAD_DIM
    half = DIFF_ROT_DIM // 2
    cd = jnp.cos(ang_d)
    sd = jnp.sin(ang_d)
    dc = jnp.where(ch < DIFF_ROT_DIM, cd, 1.0)
    ds1 = jnp.where(ch < half, -sd, 0.0)
    ds2 = jnp.where((ch >= half) & (ch < DIFF_ROT_DIM), sd, 0.0)

    def section(j):
        lo = j * D_MODEL
        return lo, jnp.dot(h, w_ref[:, lo:lo + D_MODEL], preferred_element_type=F32)

    def ret_rope(j, mult):
        base, acc = section(j)
        hw = RET_HEAD_DIM // 2
        for hd in range(RET_HEADS):
            lo = hd * RET_HEAD_DIM
            x1 = acc[:, lo:lo + hw]
            x2 = acc[:, lo + hw:lo + 2 * hw]
            o_ref[:, base + lo:base + lo + hw] = ((x1 * rc - x2 * rs) * mult).astype(BF16)
            o_ref[:, base + lo + hw:base + lo + 2 * hw] = ((x2 * rc + x1 * rs) * mult).astype(BF16)

    def diff_rope(j, mult):
        base, acc = section(j)
        for sl in range(D_MODEL // LANES):
            xs = acc[:, sl * LANES:(sl + 1) * LANES]
            up = pltpu.roll(xs, LANES - half, axis=1)
            dn = pltpu.roll(xs, half, axis=1)
            o_ref[:, base + sl * LANES:base + (sl + 1) * LANES] = (
                (xs * dc + up * ds1 + dn * ds2) * mult).astype(BF16)

    def plain(j):
        base, acc = section(j)
        o_ref[:, base:base + D_MODEL] = acc.astype(BF16)

    ret_rope(0, 1.0)
    ret_rope(1, RET_HEAD_DIM ** -0.5)
    plain(2)
    plain(3)
    diff_rope(4, DIFF_HEAD_DIM ** -0.5 * LOG2E)
    diff_rope(5, 1.0)
    for j in range(6, N_SECTIONS):
        plain(j)


def _in_proj(x2, pos2, shift3, scale3, g2, freq_r, freq_d, w_bf16, *, seq, tm):
    m, d = x2.shape
    n = w_bf16.shape[1]
    per_b = seq // tm
    return pl.pallas_call(
        _in_proj_kernel,
        out_shape=jax.ShapeDtypeStruct((m, n), BF16),
        grid=(m // tm,),
        in_specs=[
            pl.BlockSpec((tm, d), lambda i: (i, 0)),
            pl.BlockSpec((tm, 1), lambda i: (i, 0)),
            pl.BlockSpec((1, 1, d), lambda i: (i // per_b, 0, 0)),
            pl.BlockSpec((1, 1, d), lambda i: (i // per_b, 0, 0)),
            pl.BlockSpec((1, d), lambda i: (0, 0)),
            pl.BlockSpec((1, LANES), lambda i: (0, 0)),
            pl.BlockSpec((1, LANES), lambda i: (0, 0)),
            pl.BlockSpec((d, n), lambda i: (0, 0), pipeline_mode=pl.Buffered(1)),
        ],
        out_specs=pl.BlockSpec((tm, n), lambda i: (i, 0)),
        compiler_params=pltpu.CompilerParams(
            dimension_semantics=("arbitrary",),
            vmem_limit_bytes=VMEM_LIMIT_BYTES),
        name="in_proj",
    )(x2, pos2, shift3, scale3, g2, freq_r, freq_d, w_bf16)


def _log_gammas():
    lo, hi = math.log(1.0 / 32), math.log(1.0 / 512)
    return [math.log1p(-math.exp(lo + (hi - lo) * i / (RET_HEADS - 1))) for i in range(RET_HEADS)]


def _retention_kernel(q_ref, k_ref, v_ref, g_ref, ng_ref, o_ref,
                      state_ref, decay_ref, xi_ref, zeta_ref, *, tb):
    n = pl.program_id(1)

    @pl.when((pl.program_id(0) == 0) & (n == 0))
    def _tables():
        row = lax.broadcasted_iota(jnp.int32, (tb, tb), 0)
        col = lax.broadcasted_iota(jnp.int32, (tb, tb), 1)
        dist = jnp.abs(row - col).astype(F32)
        visible = (col // CHUNK) <= (row // CHUNK)
        t = lax.broadcasted_iota(jnp.int32, (tb, RET_HEAD_DIM), 0).astype(F32)
        for hd, lg in enumerate(_log_gammas()):
            decay_ref[hd] = jnp.where(visible, jnp.exp(dist * lg), 0.0)
            xi_ref[hd] = jnp.exp((t + 1.0) * lg).astype(BF16)
            zeta_ref[hd] = jnp.exp((tb - 1.0 - t) * lg).astype(BF16)

    @pl.when(n == 0)
    def _reset():
        state_ref[...] = jnp.zeros_like(state_ref)

    for hd, lg in enumerate(_log_gammas()):
        lo = hd * RET_HEAD_DIM
        q = q_ref[:, lo:lo + RET_HEAD_DIM]
        k = k_ref[:, lo:lo + RET_HEAD_DIM]
        v = v_ref[:, lo:lo + RET_HEAD_DIM]
        scores = lax.dot_general(q, k, (((1,), (1,)), ((), ())), preferred_element_type=F32)
        scores = (scores * decay_ref[hd]).astype(BF16)
        state = state_ref[hd]
        o = (jnp.dot(scores, v, preferred_element_type=F32)
             + jnp.dot(q * xi_ref[hd], state.astype(BF16), preferred_element_type=F32))
        upd = lax.dot_general(k * zeta_ref[hd], v, (((0,), (0,)), ((), ())),
                              preferred_element_type=F32)
        state_ref[hd] = state * math.exp(tb * lg) + upd

        mu = jnp.mean(o, axis=-1, keepdims=True)
        oc = o - mu
        var = jnp.mean(oc * oc, axis=-1, keepdims=True)
        o = oc * lax.rsqrt(var + GN_EPS) * ng_ref[:, lo:lo + RET_HEAD_DIM]
        gate = g_ref[:, lo:lo + RET_HEAD_DIM].astype(F32)
        o_ref[:, lo:lo + RET_HEAD_DIM] = (o * _silu(gate)).astype(BF16)


def _retention(proj, ret_norm_g2, *, bsz, seq, tb):
    m = proj.shape[0]
    nb = seq // tb
    d = D_MODEL

    def sec(k):
        return pl.BlockSpec((tb, d), lambda b, n: (b * nb + n, k))

    return pl.pallas_call(
        functools.partial(_retention_kernel, tb=tb),
        out_shape=jax.ShapeDtypeStruct((m, d), BF16),
        grid=(bsz, nb),
        in_specs=[sec(0), sec(1), sec(2), sec(3), pl.BlockSpec((1, d), lambda b, n: (0, 0))],
        out_specs=pl.BlockSpec((tb, d), lambda b, n: (b * nb + n, 0)),
        scratch_shapes=[pltpu.VMEM((RET_HEADS, RET_HEAD_DIM, RET_HEAD_DIM), F32),
                        pltpu.VMEM((RET_HEADS, tb, tb), F32),
                        pltpu.VMEM((RET_HEADS, tb, RET_HEAD_DIM), BF16),
                        pltpu.VMEM((RET_HEADS, tb, RET_HEAD_DIM), BF16)],
        compiler_params=pltpu.CompilerParams(
            dimension_semantics=("arbitrary", "arbitrary"),
            vmem_limit_bytes=VMEM_LIMIT_BYTES),
        name="retention",
    )(proj, proj, proj, proj, ret_norm_g2)


def _diff_attn_kernel(q_ref, k_ref, v_ref, g_ref, lq1_ref, lk1_ref, lq2_ref, lk2_ref, sg_ref,
                      o_ref, acc_ref, m_ref, l_ref, *, seq, tile, lambda_init):
    hd = DIFF_HEAD_DIM
    n_tiles = seq // tile
    nt = (((1,), (1,)), ((), ()))
    tn = (((0,), (0,)), ((), ()))

    lam = (jnp.exp(jnp.sum(lq1_ref[...] * lk1_ref[...], axis=-1, keepdims=True))
           - jnp.exp(jnp.sum(lq2_ref[...] * lk2_ref[...], axis=-1, keepdims=True))
           + lambda_init)

    def epilogue(o_t, gate):
        o = o_t.T
        o = o * lax.rsqrt(jnp.mean(o * o, axis=-1, keepdims=True) + RMS_EPS)
        o = o * sg_ref[...] * (1.0 - lambda_init)
        return (o * _silu(gate.astype(F32))).astype(BF16)

    lane = lax.broadcasted_iota(jnp.int32, (tile, 2 * hd), 1)
    row_chunk = lax.broadcasted_iota(jnp.int32, (tile, tile), 0) // CHUNK
    col_chunk = lax.broadcasted_iota(jnp.int32, (tile, tile), 1) // CHUNK
    visible = row_chunk <= col_chunk
    one_hot0 = jnp.where(lane == 0, 1.0, 0.0).astype(BF16)

    def diagonal(i):
        q = q_ref[i * tile:(i + 1) * tile, :]
        k = k_ref[i * tile:(i + 1) * tile, :]
        v = v_ref[i * tile:(i + 1) * tile, :]
        zero = jnp.zeros_like(q)
        q_aug, state = [], []
        for qm in (jnp.where(lane < hd, q, zero), jnp.where(lane >= hd, q, zero)):
            s = lax.dot_general(k, qm, nt, preferred_element_type=F32)
            s = jnp.where(visible, s, -jnp.inf)
            r = jnp.max(s, axis=0, keepdims=True).astype(BF16).astype(F32)
            p = jnp.exp2(s - r)
            l = jnp.sum(p, axis=0, keepdims=True)
            acc = lax.dot_general(v, p.astype(BF16), tn, preferred_element_type=F32)
            neg_r = jnp.broadcast_to(-r, (2 * hd, tile)).T
            ext = jnp.where(lane == 0, neg_r, 0.0).astype(BF16)
            q_aug.append(jnp.concatenate([qm, ext], axis=1))
            state.append((l, acc))
        return q_aug, state

    def full_scores(j, q_aug):
        k_aug = jnp.concatenate([k_ref[j * tile:(j + 1) * tile, :], one_hot0], axis=1)
        return [lax.dot_general(k_aug, qa, nt, preferred_element_type=F32) for qa in q_aug]

    def full_consume(j, shifted, state):
        v = v_ref[j * tile:(j + 1) * tile, :]
        new_state = []
        for s, (l, acc) in zip(shifted, state):
            p = jnp.exp2(s)
            l = l + jnp.sum(p, axis=0, keepdims=True)
            acc = acc + lax.dot_general(v, p.astype(BF16), tn, preferred_element_type=F32)
            new_state.append((l, acc))
        return new_state

    bad = jnp.zeros((1, tile), F32)
    for i in range(n_tiles):
        q_aug, state = diagonal(i)
        pending = full_scores(0, q_aug) if i > 0 else None
        for j in range(i):
            nxt = full_scores(j + 1, q_aug) if j + 1 < i else None
            state = full_consume(j, pending, state)
            pending = nxt
        (l0, acc0), (l1, acc1) = state
        o_t = acc0 / l0 - lam * (acc1 / l1)
        bad = jnp.maximum(bad, jnp.where(jnp.isfinite(l0) & jnp.isfinite(l1), 0.0, 1.0))
        bad = jnp.maximum(bad, jnp.max(jnp.where(jnp.isfinite(o_t), 0.0, 1.0), axis=0, keepdims=True))
        o_ref[i * tile:(i + 1) * tile, :] = epilogue(o_t, g_ref[i * tile:(i + 1) * tile, :])

    @pl.when(jnp.max(bad) > 0.0)
    def _redo():
        def q_body(i, carry):
            qs = pl.multiple_of(i * tile, tile)
            q = q_ref[pl.ds(qs, tile), :]
            zero = jnp.zeros_like(q)
            qms = (jnp.where(lane < hd, q, zero), jnp.where(lane >= hd, q, zero))
            m_ref[...] = jnp.full(m_ref.shape, -1e30, F32)
            l_ref[...] = jnp.zeros_like(l_ref)
            acc_ref[...] = jnp.zeros_like(acc_ref)

            def kv_body(j, c):
                ks = pl.multiple_of(j * tile, tile)
                k = k_ref[pl.ds(ks, tile), :]
                v = v_ref[pl.ds(ks, tile), :]
                vis = (row_chunk + j * (tile // CHUNK)) <= (col_chunk + i * (tile // CHUNK))
                for mp in range(2):
                    s = lax.dot_general(k, qms[mp], nt, preferred_element_type=F32)
                    s = jnp.where(vis, s, -jnp.inf)
                    m_old = m_ref[mp]
                    m_new = jnp.maximum(m_old, jnp.max(s, axis=0, keepdims=True))
                    alpha = jnp.exp2(m_old - m_new)
                    p = jnp.exp2(s - m_new)
                    l_ref[mp] = alpha * l_ref[mp] + jnp.sum(p, axis=0, keepdims=True)
                    m_ref[mp] = m_new
                    acc_ref[mp] = alpha * acc_ref[mp] + lax.dot_general(
                        v, p.astype(BF16), tn, preferred_element_type=F32)
                return c

            lax.fori_loop(0, i + 1, kv_body, 0)
            o_t = acc_ref[0] / l_ref[0] - lam * (acc_ref[1] / l_ref[1])
            o_ref[pl.ds(qs, tile), :] = epilogue(o_t, g_ref[pl.ds(qs, tile), :])
            return carry

        lax.fori_loop(0, n_tiles, q_body, 0)


def _diff_attn(proj, lq1, lk1, lq2, lk2, subln_g2, *, bsz, seq, tile, lambda_init):
    m = proj.shape[0]
    w = 2 * DIFF_HEAD_DIM
    per_sec = D_MODEL // w

    def head_cols(sec_idx):
        return pl.BlockSpec((seq, w), lambda b, h: (b, sec_idx * per_sec + h))

    small = pl.BlockSpec((1, DIFF_HEAD_DIM), lambda b, h: (0, 0))
    return pl.pallas_call(
        functools.partial(_diff_attn_kernel, seq=seq, tile=tile, lambda_init=lambda_init),
        out_shape=jax.ShapeDtypeStruct((m, D_MODEL), BF16),
        grid=(bsz, DIFF_HEADS),
        in_specs=[head_cols(4), head_cols(5), head_cols(6), head_cols(7),
                  small, small, small, small, pl.BlockSpec((1, w), lambda b, h: (0, 0))],
        out_specs=pl.BlockSpec((seq, w), lambda b, h: (b, h)),
        scratch_shapes=[pltpu.VMEM((2, w, tile), F32), pltpu.VMEM((2, 1, tile), F32),
                        pltpu.VMEM((2, 1, tile), F32)],
        compiler_params=pltpu.CompilerParams(
            dimension_semantics=("arbitrary", "arbitrary"),
            vmem_limit_bytes=VMEM_LIMIT_BYTES),
        name="diff_attn",
    )(proj, proj, proj, proj, lq1, lk1, lq2, lk2, subln_g2)


def _out_kernel(x_ref, yr_ref, yd_ref, mr_ref, md_ref, gate_ref, pg_ref, wr_ref, wd_ref, wo_ref, o_ref):
    br = jnp.dot(yr_ref[...], wr_ref[...], preferred_element_type=F32)
    bd = jnp.dot(yd_ref[...], wd_ref[...], preferred_element_type=F32)
    merged = (jax.nn.sigmoid(mr_ref[...].astype(F32)) * br
              + jax.nn.sigmoid(md_ref[...].astype(F32)) * bd)
    out = jnp.dot(merged.astype(BF16), wo_ref[...], preferred_element_type=F32)
    normed = out * lax.rsqrt(jnp.mean(out * out, axis=-1, keepdims=True) + RMS_EPS) * pg_ref[...]
    o_ref[...] = x_ref[...] + gate_ref[0] * normed


def _out_merge(x2, y_ret, y_diff, proj, gate3, post_g2, wr, wd, wo, *, seq, tm):
    m, d = x2.shape
    per_b = seq // tm
    row = lambda k: pl.BlockSpec((tm, d), lambda i: (i, k))
    full = pl.BlockSpec((d, d), lambda i: (0, 0))
    return pl.pallas_call(
        _out_kernel,
        out_shape=jax.ShapeDtypeStruct((m, d), F32),
        grid=(m // tm,),
        in_specs=[row(0), row(0), row(0), row(8), row(9),
                  pl.BlockSpec((1, 1, d), lambda i: (i // per_b, 0, 0)),
                  pl.BlockSpec((1, d), lambda i: (0, 0)),
                  full, full, full],
        out_specs=row(0),
        compiler_params=pltpu.CompilerParams(
            dimension_semantics=("arbitrary",),
            vmem_limit_bytes=VMEM_LIMIT_BYTES),
        name="out_merge",
    )(x2, y_ret, y_diff, proj, proj, gate3, post_g2, wr, wd, wo)


def _rope_freqs():
    half_r = RET_HEAD_DIM // 2
    fr = RET_THETA ** (-jnp.arange(half_r, dtype=F32) * 2.0 / RET_HEAD_DIM)
    half_d = DIFF_ROT_DIM // 2
    fd = ROPE_THETA ** (-jnp.arange(half_d, dtype=F32) * 2.0 / DIFF_ROT_DIM)
    ch = np.arange(LANES) % DIFF_HEAD_DIM
    fd_lane = jnp.where(ch < DIFF_ROT_DIM, fd[ch % half_d], 0.0).astype(F32)
    return fr.reshape(1, half_r), fd_lane.reshape(1, LANES)


def kernel(x, c, positions, ada_w, ada_b, pre_norm_g, w_in, ret_norm_g, diff_lambda_q1, diff_lambda_k1, diff_lambda_q2, diff_lambda_k2, diff_subln_g, w_branch_ret, w_branch_diff, w_out, post_norm_g):
    bsz, seq, d = x.shape
    depth = ada_w.shape[0]
    m = bsz * seq
    freq_r, freq_d = _rope_freqs()
    pos2 = positions.reshape(m, 1)
    x2 = x.reshape(m, d)
    for l in range(depth):
        lambda_init = 0.8 - 0.6 * math.exp(-0.3 * l)
        mod = _ada_mod(c, ada_w[l].astype(BF16), ada_b[l].reshape(1, 3 * d))
        shift3 = mod[:, :d].reshape(bsz, 1, d)
        scale3 = mod[:, d:2 * d].reshape(bsz, 1, d)
        gate3 = mod[:, 2 * d:].reshape(bsz, 1, d)
        proj = _in_proj(x2, pos2, shift3, scale3, pre_norm_g[l].reshape(1, d), freq_r, freq_d,
                        w_in[l].astype(BF16), seq=seq, tm=512)
        y_ret = _retention(proj, ret_norm_g[l].reshape(1, d), bsz=bsz, seq=seq, tb=512)
        y_diff = _diff_attn(proj,
                            diff_lambda_q1[l].reshape(1, -1), diff_lambda_k1[l].reshape(1, -1),
                            diff_lambda_q2[l].reshape(1, -1), diff_lambda_k2[l].reshape(1, -1),
                            diff_subln_g[l].reshape(1, -1),
                            bsz=bsz, seq=seq, tile=512, lambda_init=lambda_init)
        x2 = _out_merge(x2, y_ret, y_diff, proj, gate3, post_norm_g[l].reshape(1, d),
                        w_branch_ret[l].astype(BF16), w_branch_diff[l].astype(BF16),
                        w_out[l].astype(BF16), seq=seq, tm=512)
    return x2.reshape(bsz, seq, d)
```

```python
import functools
import math

import jax
import jax.numpy as jnp
import numpy as np
from jax import lax
from jax.experimental import pallas as pl
from jax.experimental.pallas import tpu as pltpu

D_MODEL = 1024
CHUNK = 64
RMS_EPS = 1e-6
GN_EPS = 1e-5
RET_HEAD_DIM = 256
RET_HEADS = D_MODEL // RET_HEAD_DIM
RET_THETA = 10000.0
DIFF_HEAD_DIM = 64
DIFF_HEADS = D_MODEL // (2 * DIFF_HEAD_DIM)
DIFF_ROT_DIM = DIFF_HEAD_DIM // 4
ROPE_THETA = 500000.0
N_SECTIONS = 10
LANES = 128
LOG2E = math.log2(math.e)

VMEM_LIMIT_BYTES = 56 * 1024 * 1024

BF16 = jnp.bfloat16
F32 = jnp.float32


def _silu(v):
    return v * jax.nn.sigmoid(v)


def _ada_kernel(c_ref, w_ref, b_ref, o_ref):
    sc = _silu(c_ref[...]).astype(BF16)
    o_ref[...] = jnp.dot(sc, w_ref[...].astype(BF16), preferred_element_type=F32) + b_ref[...]


def _ada_mod(c, w, b):
    bsz, d = c.shape
    n = w.shape[1]
    tn = 1024
    return pl.pallas_call(
        _ada_kernel,
        out_shape=jax.ShapeDtypeStruct((bsz, n), F32),
        grid=(n // tn,),
        in_specs=[
            pl.BlockSpec((bsz, d), lambda j: (0, 0)),
            pl.BlockSpec((d, tn), lambda j: (0, j)),
            pl.BlockSpec((1, tn), lambda j: (0, j)),
        ],
        out_specs=pl.BlockSpec((bsz, tn), lambda j: (0, j)),
        name="ada_mod",
    )(c, w, b)


def _in_proj_kernel(x_ref, pos_ref, shift_ref, scale_ref, g_ref, fr_ref, fd_ref, w_ref, o_ref):
    x = x_ref[...]
    inv = lax.rsqrt(jnp.mean(x * x, axis=-1, keepdims=True) + RMS_EPS)
    h = (x * inv) * g_ref[...]
    h = (h * (1.0 + scale_ref[0]) + shift_ref[0]).astype(BF16)

    pos = pos_ref[...].astype(F32)
    ang_r = pos * fr_ref[...]
    rc = jnp.cos(ang_r)
    rs = jnp.sin(ang_r)
    ang_d = pos * fd_ref[...]
    ch = lax.broadcasted_iota(jnp.int32, ang_d.shape, 1) % DIFF_HEAD_DIM
    half = DIFF_ROT_DIM // 2
    cd = jnp.cos(ang_d)
    sd = jnp.sin(ang_d)
    dc = jnp.where(ch < DIFF_ROT_DIM, cd, 1.0)
    ds1 = jnp.where(ch < half, -sd, 0.0)
    ds2 = jnp.where((ch >= half) & (ch < DIFF_ROT_DIM), sd, 0.0)

    def section(j):
        lo = j * D_MODEL
        return lo, jnp.dot(h, w_ref[:, lo:lo + D_MODEL], preferred_element_type=F32)

    def ret_rope(j, mult):
        base, acc = section(j)
        hw = RET_HEAD_DIM // 2
        for hd in range(RET_HEADS):
            lo = hd * RET_HEAD_DIM
            x1 = acc[:, lo:lo + hw]
            x2 = acc[:, lo + hw:lo + 2 * hw]
            o_ref[:, base + lo:base + lo + hw] = ((x1 * rc - x2 * rs) * mult).astype(BF16)
            o_ref[:, base + lo + hw:base + lo + 2 * hw] = ((x2 * rc + x1 * rs) * mult).astype(BF16)

    def diff_rope(j, mult):
        base, acc = section(j)
        for sl in range(D_MODEL // LANES):
            xs = acc[:, sl * LANES:(sl + 1) * LANES]
            up = pltpu.roll(xs, LANES - half, axis=1)
            dn = pltpu.roll(xs, half, axis=1)
            o_ref[:, base + sl * LANES:base + (sl + 1) * LANES] = (
                (xs * dc + up * ds1 + dn * ds2) * mult).astype(BF16)

    def plain(j):
        base, acc = section(j)
        o_ref[:, base:base + D_MODEL] = acc.astype(BF16)

    ret_rope(0, 1.0)
    ret_rope(1, RET_HEAD_DIM ** -0.5)
    plain(2)
    plain(3)
    diff_rope(4, DIFF_HEAD_DIM ** -0.5 * LOG2E)
    diff_rope(5, 1.0)
    for j in range(6, N_SECTIONS):
        plain(j)


def _in_proj(x2, pos2, shift3, scale3, g2, freq_r, freq_d, w_bf16, *, seq, tm):
    m, d = x2.shape
    n = w_bf16.shape[1]
    per_b = seq // tm
    return pl.pallas_call(
        _in_proj_kernel,
        out_shape=jax.ShapeDtypeStruct((m, n), BF16),
        grid=(m // tm,),
        in_specs=[
            pl.BlockSpec((tm, d), lambda i: (i, 0)),
            pl.BlockSpec((tm, 1), lambda i: (i, 0)),
            pl.BlockSpec((1, 1, d), lambda i: (i // per_b, 0, 0)),
            pl.BlockSpec((1, 1, d), lambda i: (i // per_b, 0, 0)),
            pl.BlockSpec((1, d), lambda i: (0, 0)),
            pl.BlockSpec((1, LANES), lambda i: (0, 0)),
            pl.BlockSpec((1, LANES), lambda i: (0, 0)),
            pl.BlockSpec((d, n), lambda i: (0, 0), pipeline_mode=pl.Buffered(1)),
        ],
        out_specs=pl.BlockSpec((tm, n), lambda i: (i, 0)),
        compiler_params=pltpu.CompilerParams(
            dimension_semantics=("arbitrary",),
            vmem_limit_bytes=VMEM_LIMIT_BYTES),
        name="in_proj",
    )(x2, pos2, shift3, scale3, g2, freq_r, freq_d, w_bf16)


def _log_gammas():
    lo, hi = math.log(1.0 / 32), math.log(1.0 / 512)
    return [math.log1p(-math.exp(lo + (hi - lo) * i / (RET_HEADS - 1))) for i in range(RET_HEADS)]


def _retention_kernel(q_ref, k_ref, v_ref, g_ref, ng_ref, o_ref,
                      state_ref, decay_ref, xi_ref, zeta_ref, *, tb):
    n = pl.program_id(1)

    @pl.when((pl.program_id(0) == 0) & (n == 0))
    def _tables():
        row = lax.broadcasted_iota(jnp.int32, (tb, tb), 0)
        col = lax.broadcasted_iota(jnp.int32, (tb, tb), 1)
        dist = jnp.abs(row - col).astype(F32)
        visible = (col // CHUNK) <= (row // CHUNK)
        t = lax.broadcasted_iota(jnp.int32, (tb, RET_HEAD_DIM), 0).astype(F32)
        for hd, lg in enumerate(_log_gammas()):
            decay_ref[hd] = jnp.where(visible, jnp.exp(dist * lg), 0.0)
            xi_ref[hd] = jnp.exp((t + 1.0) * lg).astype(BF16)
            zeta_ref[hd] = jnp.exp((tb - 1.0 - t) * lg).astype(BF16)

    @pl.when(n == 0)
    def _reset():
        state_ref[...] = jnp.zeros_like(state_ref)

    for hd, lg in enumerate(_log_gammas()):
        lo = hd * RET_HEAD_DIM
        q = q_ref[:, lo:lo + RET_HEAD_DIM]
        k = k_ref[:, lo:lo + RET_HEAD_DIM]
        v = v_ref[:, lo:lo + RET_HEAD_DIM]
        scores = lax.dot_general(q, k, (((1,), (1,)), ((), ())), preferred_element_type=F32)
        scores = (scores * decay_ref[hd]).astype(BF16)
        state = state_ref[hd]
        o = (jnp.dot(scores, v, preferred_element_type=F32)
             + jnp.dot(q * xi_ref[hd], state.astype(BF16), preferred_element_type=F32))
        upd = lax.dot_general(k * zeta_ref[hd], v, (((0,), (0,)), ((), ())),
                              preferred_element_type=F32)
        state_ref[hd] = state * math.exp(tb * lg) + upd

        mu = jnp.mean(o, axis=-1, keepdims=True)
        oc = o - mu
        var = jnp.mean(oc * oc, axis=-1, keepdims=True)
        o = oc * lax.rsqrt(var + GN_EPS) * ng_ref[:, lo:lo + RET_HEAD_DIM]
        gate = g_ref[:, lo:lo + RET_HEAD_DIM].astype(F32)
        o_ref[:, lo:lo + RET_HEAD_DIM] = (o * _silu(gate)).astype(BF16)


def _retention(proj, ret_norm_g2, *, bsz, seq, tb):
    m = proj.shape[0]
    nb = seq // tb
    d = D_MODEL

    def sec(k):
        return pl.BlockSpec((tb, d), lambda b, n: (b * nb + n, k))

    return pl.pallas_call(
        functools.partial(_retention_kernel, tb=tb),
        out_shape=jax.ShapeDtypeStruct((m, d), BF16),
        grid=(bsz, nb),
        in_specs=[sec(0), sec(1), sec(2), sec(3), pl.BlockSpec((1, d), lambda b, n: (0, 0))],
        out_specs=pl.BlockSpec((tb, d), lambda b, n: (b * nb + n, 0)),
        scratch_shapes=[pltpu.VMEM((RET_HEADS, RET_HEAD_DIM, RET_HEAD_DIM), F32),
                        pltpu.VMEM((RET_HEADS, tb, tb), F32),
                        pltpu.VMEM((RET_HEADS, tb, RET_HEAD_DIM), BF16),
                        pltpu.VMEM((RET_HEADS, tb, RET_HEAD_DIM), BF16)],
        compiler_params=pltpu.CompilerParams(
            dimension_semantics=("arbitrary", "arbitrary"),
            vmem_limit_bytes=VMEM_LIMIT_BYTES),
        name="retention",
    )(proj, proj, proj, proj, ret_norm_g2)


def _diff_attn_kernel(q_ref, k_ref, v_ref, g_ref, lq1_ref, lk1_ref, lq2_ref, lk2_ref, sg_ref,
                      o_ref, acc_ref, m_ref, l_ref, vt_ref, *, seq, tile, lambda_init):
    hd = DIFF_HEAD_DIM
    n_tiles = seq // tile
    nt = (((1,), (1,)), ((), ()))
    tn = (((0,), (0,)), ((), ()))

    lam = (jnp.exp(jnp.sum(lq1_ref[...] * lk1_ref[...], axis=-1, keepdims=True))
           - jnp.exp(jnp.sum(lq2_ref[...] * lk2_ref[...], axis=-1, keepdims=True))
           + lambda_init)

    def epilogue(o_t, gate):
        o = o_t.T
        o = o * lax.rsqrt(jnp.mean(o * o, axis=-1, keepdims=True) + RMS_EPS)
        o = o * sg_ref[...] * (1.0 - lambda_init)
        return (o * _silu(gate.astype(F32))).astype(BF16)

    lane = lax.broadcasted_iota(jnp.int32, (tile, 2 * hd), 1)
    row_chunk = lax.broadcasted_iota(jnp.int32, (tile, tile), 0) // CHUNK
    col_chunk = lax.broadcasted_iota(jnp.int32, (tile, tile), 1) // CHUNK
    visible = row_chunk <= col_chunk
    one_hot0 = jnp.where(lane == 0, 1.0, 0.0).astype(BF16)

    vt_ref[...] = v_ref[...].T

    def diagonal_front(i):
        q = q_ref[i * tile:(i + 1) * tile, :]
        k = k_ref[i * tile:(i + 1) * tile, :]
        zero = jnp.zeros_like(q)
        q_aug, parts = [], []
        for qm in (jnp.where(lane < hd, q, zero), jnp.where(lane >= hd, q, zero)):
            s = lax.dot_general(k, qm, nt, preferred_element_type=F32)
            s = jnp.where(visible, s, -jnp.inf)
            r = jnp.max(s, axis=0, keepdims=True).astype(BF16).astype(F32)
            p = jnp.exp2(s - r)
            neg_r = jnp.broadcast_to(-r, (2 * hd, tile)).T
            ext = jnp.where(lane == 0, neg_r, 0.0).astype(BF16)
            q_aug.append(jnp.concatenate([qm, ext], axis=1))
            parts.append((jnp.sum(p, axis=0, keepdims=True), p.astype(BF16)))
        return q_aug, parts

    def diagonal_back(i, parts):
        vt = vt_ref[:, i * tile:(i + 1) * tile]
        return [(l, jnp.dot(vt, p, preferred_element_type=F32)) for l, p in parts]

    def full_scores(j, q_aug):
        k_aug = jnp.concatenate([k_ref[j * tile:(j + 1) * tile, :], one_hot0], axis=1)
        return [lax.dot_general(k_aug, qa, nt, preferred_element_type=F32) for qa in q_aug]

    def full_consume(j, shifted, state):
        vt = vt_ref[:, j * tile:(j + 1) * tile]
        new_state = []
        for s, (l, acc) in zip(shifted, state):
            p = jnp.exp2(s)
            l = l + jnp.sum(p, axis=0, keepdims=True)
            acc = acc + jnp.dot(vt, p.astype(BF16), preferred_element_type=F32)
            new_state.append((l, acc))
        return new_state

    bad = jnp.zeros((1, tile), F32)
    q_aug, parts = diagonal_front(0)
    state = diagonal_back(0, parts)
    for i in range(n_tiles):
        if i + 1 < n_tiles:
            q_aug_next, parts_next = diagonal_front(i + 1)
        pending = full_scores(0, q_aug) if i > 0 else None
        for j in range(i):
            nxt = full_scores(j + 1, q_aug) if j + 1 < i else None
            state = full_consume(j, pending, state)
            pending = nxt
        if i + 1 < n_tiles:
            state_next = diagonal_back(i + 1, parts_next)
        (l0, acc0), (l1, acc1) = state
        o_t = acc0 / l0 - lam * (acc1 / l1)
        bad = jnp.maximum(bad, jnp.where(jnp.isfinite(l0) & jnp.isfinite(l1), 0.0, 1.0))
        bad = jnp.maximum(bad, jnp.max(jnp.where(jnp.isfinite(o_t), 0.0, 1.0), axis=0, keepdims=True))
        o_ref[i * tile:(i + 1) * tile, :] = epilogue(o_t, g_ref[i * tile:(i + 1) * tile, :])
        if i + 1 < n_tiles:
            q_aug, state = q_aug_next, state_next

    @pl.when(jnp.max(bad) > 0.0)
    def _redo():
        def q_body(i, carry):
            qs = pl.multiple_of(i * tile, tile)
            q = q_ref[pl.ds(qs, tile), :]
            zero = jnp.zeros_like(q)
            qms = (jnp.where(lane < hd, q, zero), jnp.where(lane >= hd, q, zero))
            m_ref[...] = jnp.full(m_ref.shape, -1e30, F32)
            l_ref[...] = jnp.zeros_like(l_ref)
            acc_ref[...] = jnp.zeros_like(acc_ref)

            def kv_body(j, c):
                ks = pl.multiple_of(j * tile, tile)
                k = k_ref[pl.ds(ks, tile), :]
                v = v_ref[pl.ds(ks, tile), :]
                vis = (row_chunk + j * (tile // CHUNK)) <= (col_chunk + i * (tile // CHUNK))
                for mp in range(2):
                    s = lax.dot_general(k, qms[mp], nt, preferred_element_type=F32)
                    s = jnp.where(vis, s, -jnp.inf)
                    m_old = m_ref[mp]
                    m_new = jnp.maximum(m_old, jnp.max(s, axis=0, keepdims=True))
                    alpha = jnp.exp2(m_old - m_new)
                    p = jnp.exp2(s - m_new)
                    l_ref[mp] = alpha * l_ref[mp] + jnp.sum(p, axis=0, keepdims=True)
                    m_ref[mp] = m_new
                    acc_ref[mp] = alpha * acc_ref[mp] + lax.dot_general(
                        v, p.astype(BF16), tn, preferred_element_type=F32)
                return c

            lax.fori_loop(0, i + 1, kv_body, 0)
            o_t = acc_ref[0] / l_ref[0] - lam * (acc_ref[1] / l_ref[1])
            o_ref[pl.ds(qs, tile), :] = epilogue(o_t, g_ref[pl.ds(qs, tile), :])
            return carry

        lax.fori_loop(0, n_tiles, q_body, 0)


def _diff_attn(proj, lq1, lk1, lq2, lk2, subln_g2, *, bsz, seq, tile, lambda_init):
    m = proj.shape[0]
    w = 2 * DIFF_HEAD_DIM
    per_sec = D_MODEL // w

    def head_cols(sec_idx):
        return pl.BlockSpec((seq, w), lambda b, h: (b, sec_idx * per_sec + h))

    small = pl.BlockSpec((1, DIFF_HEAD_DIM), lambda b, h: (0, 0))
    return pl.pallas_call(
        functools.partial(_diff_attn_kernel, seq=seq, tile=tile, lambda_init=lambda_init),
        out_shape=jax.ShapeDtypeStruct((m, D_MODEL), BF16),
        grid=(bsz, DIFF_HEADS),
        in_specs=[head_cols(4), head_cols(5), head_cols(6), head_cols(7),
                  small, small, small, small, pl.BlockSpec((1, w), lambda b, h: (0, 0))],
        out_specs=pl.BlockSpec((seq, w), lambda b, h: (b, h)),
        scratch_shapes=[pltpu.VMEM((2, w, tile), F32), pltpu.VMEM((2, 1, tile), F32),
                        pltpu.VMEM((2, 1, tile), F32), pltpu.VMEM((w, seq), BF16)],
        compiler_params=pltpu.CompilerParams(
            dimension_semantics=("arbitrary", "arbitrary"),
            vmem_limit_bytes=VMEM_LIMIT_BYTES),
        name="diff_attn",
    )(proj, proj, proj, proj, lq1, lk1, lq2, lk2, subln_g2)


OUT_SUB_BLOCKS = 2


def _out_kernel(x_ref, yr_ref, yd_ref, mr_ref, md_ref, gate_ref, pg_ref, wr_ref, wd_ref, wo_ref,
                o_ref, w_ref):
    @pl.when(pl.program_id(0) == 0)
    def _cast_weights():
        for idx, src in enumerate((wr_ref, wd_ref, wo_ref)):
            w_ref[idx] = src[...].astype(BF16)

    rows = x_ref.shape[0] // OUT_SUB_BLOCKS
    for sb in range(OUT_SUB_BLOCKS):
        rs = slice(sb * rows, (sb + 1) * rows)
        br = jnp.dot(yr_ref[rs, :], w_ref[0], preferred_element_type=F32)
        bd = jnp.dot(yd_ref[rs, :], w_ref[1], preferred_element_type=F32)
        merged = (jax.nn.sigmoid(mr_ref[rs, :].astype(F32)) * br
                  + jax.nn.sigmoid(md_ref[rs, :].astype(F32)) * bd)
        out = jnp.dot(merged.astype(BF16), w_ref[2], preferred_element_type=F32)
        normed = out * lax.rsqrt(jnp.mean(out * out, axis=-1, keepdims=True) + RMS_EPS) * pg_ref[...]
        o_ref[rs, :] = x_ref[rs, :] + gate_ref[0] * normed


def _out_merge(x2, y_ret, y_diff, proj, gate3, post_g2, wr, wd, wo, *, seq, tm):
    m, d = x2.shape
    per_b = seq // tm
    row = lambda k: pl.BlockSpec((tm, d), lambda i: (i, k))
    full = pl.BlockSpec((d, d), lambda i: (0, 0), pipeline_mode=pl.Buffered(1))
    return pl.pallas_call(
        _out_kernel,
        out_shape=jax.ShapeDtypeStruct((m, d), F32),
        grid=(m // tm,),
        in_specs=[row(0), row(0), row(0), row(8), row(9),
                  pl.BlockSpec((1, 1, d), lambda i: (i // per_b, 0, 0)),
                  pl.BlockSpec((1, d), lambda i: (0, 0)),
                  full, full, full],
        out_specs=row(0),
        scratch_shapes=[pltpu.VMEM((3, d, d), BF16)],
        compiler_params=pltpu.CompilerParams(
            dimension_semantics=("arbitrary",),
            vmem_limit_bytes=VMEM_LIMIT_BYTES),
        name="out_merge",
    )(x2, y_ret, y_diff, proj, proj, gate3, post_g2, wr, wd, wo)


def _rope_freqs():
    half_r = RET_HEAD_DIM // 2
    fr = RET_THETA ** (-jnp.arange(half_r, dtype=F32) * 2.0 / RET_HEAD_DIM)
    half_d = DIFF_ROT_DIM // 2
    fd = ROPE_THETA ** (-jnp.arange(half_d, dtype=F32) * 2.0 / DIFF_ROT_DIM)
    ch = np.arange(LANES) % DIFF_HEAD_DIM
    fd_lane = jnp.where(ch < DIFF_ROT_DIM, fd[ch % half_d], 0.0).astype(F32)
    return fr.reshape(1, half_r), fd_lane.reshape(1, LANES)


def kernel(x, c, positions, ada_w, ada_b, pre_norm_g, w_in, ret_norm_g, diff_lambda_q1, diff_lambda_k1, diff_lambda_q2, diff_lambda_k2, diff_subln_g, w_branch_ret, w_branch_diff, w_out, post_norm_g):
    bsz, seq, d = x.shape
    depth = ada_w.shape[0]
    m = bsz * seq
    freq_r, freq_d = _rope_freqs()
    pos2 = positions.reshape(m, 1)
    x2 = x.reshape(m, d)
    for l in range(depth):
        lambda_init = 0.8 - 0.6 * math.exp(-0.3 * l)
        mod = _ada_mod(c, ada_w[l], ada_b[l].reshape(1, 3 * d))
        shift3 = mod[:, :d].reshape(bsz, 1, d)
        scale3 = mod[:, d:2 * d].reshape(bsz, 1, d)
        gate3 = mod[:, 2 * d:].reshape(bsz, 1, d)
        proj = _in_proj(x2, pos2, shift3, scale3, pre_norm_g[l].reshape(1, d), freq_r, freq_d,
                        w_in[l].astype(BF16), seq=seq, tm=512)
        y_ret = _retention(proj, ret_norm_g[l].reshape(1, d), bsz=bsz, seq=seq, tb=512)
        y_diff = _diff_attn(proj,
                            diff_lambda_q1[l].reshape(1, -1), diff_lambda_k1[l].reshape(1, -1),
                            diff_lambda_q2[l].reshape(1, -1), diff_lambda_k2[l].reshape(1, -1),
                            diff_subln_g[l].reshape(1, -1),
                            bsz=bsz, seq=seq, tile=512, lambda_init=lambda_init)
        x2 = _out_merge(x2, y_ret, y_diff, proj, gate3, post_norm_g[l].reshape(1, d),
                        w_branch_ret[l], w_branch_diff[l], w_out[l], seq=seq, tm=512)
    return x2.reshape(bsz, seq, d)
```
